```python
import math
import jax, jax.numpy as jnp
from jax import lax
import numpy as np

D_MODEL = 1024
BATCH = 8
SEQ = 2048
DEPTH = 4
DEC_BATCH = 128
DEC_SEQ = 8
PAST_LEN = 16384
PAGE_SIZE = 128

N_MIXERS = 2
N_HEADS = 4
DK = D_MODEL // 8
DV = D_MODEL // 4
QK_W = N_HEADS * DK
V_W = N_HEADS * DV
CONV_W = 4
GATE_RANK = 16
GLA_NORMALIZER = 16.0
D_FF = ((8 * D_MODEL // 3 + 255) // 256) * 256
MLSTM_CHUNK = 64
GLA_CHUNK = 32
N_MLSTM = (DEPTH + 1) // 2
N_GLA = DEPTH // 2
EPS = 1e-6

kernel_name = 'hybrid_mlstm_gla_macaron_step'


def rmsnorm(x, w):
    xf = x.astype(jnp.float32)
    y = xf * lax.rsqrt(jnp.mean(xf * xf, axis=-1, keepdims=True) + EPS)
    return (y * w.astype(jnp.float32)).astype(x.dtype)


def swiglu(x, w_gate, w_up, w_down):
    return (jax.nn.silu(x @ w_gate) * (x @ w_up)) @ w_down


def causal_conv(u, buf, w, b):
    T = u.shape[1]
    full = jnp.concatenate([buf.astype(u.dtype), u], axis=1)
    y = sum(full[:, j:j + T] * w[j] for j in range(CONV_W)) + b
    return y, full[:, -(CONV_W - 1):]


def to_chunks(a, L):
    B, T = a.shape[:2]
    return jnp.moveaxis(a.reshape((B, T // L, L) + a.shape[2:]), 1, 0)


def from_chunks(a):
    n, B, L = a.shape[:3]
    return jnp.moveaxis(a, 0, 1).reshape((B, n * L) + a.shape[3:])


def mlstm_scan(q, k, v, ig, lf, C0, n0, m0):
    T = q.shape[1]
    L = math.gcd(T, MLSTM_CHUNK)
    causal = jnp.tril(jnp.ones((L, L), dtype=bool))
    f32 = jnp.float32

    def step(carry, xs):
        C, n, m = carry
        qc, kc, vc, ic, fc = xs
        b = jnp.cumsum(fc, axis=1)
        a = b + m[:, None, :]
        Dm = b[:, :, None, :] - b[:, None, :, :] + ic[:, None, :, :]
        Dm = jnp.where(causal[None, :, :, None], Dm, -jnp.inf)
        mt = jnp.maximum(a, jnp.max(Dm, axis=2))
        w_inter = jnp.exp(a - mt)
        qk = jnp.einsum('bthk,bshk->btsh', qc, kc) * jnp.exp(Dm - mt[:, :, None, :])
        num = jnp.einsum('btsh,bshv->bthv', qk, vc) + w_inter[..., None] * jnp.einsum('bthk,bhkv->bthv', qc, C)
        den = jnp.sum(qk, axis=2) + w_inter * jnp.einsum('bthk,bhk->bth', qc, n)
        h = num / jnp.maximum(jnp.abs(den), jnp.exp(-mt))[..., None]
        bL = b[:, -1]
        g = bL[:, None, :] - b + ic
        m_new = jnp.maximum(bL + m, jnp.max(g, axis=1))
        sc_prev = jnp.exp(bL + m - m_new)
        sc = jnp.exp(g - m_new[:, None, :])
        C_new = sc_prev[..., None, None] * C + jnp.einsum('bsh,bshk,bshv->bhkv', sc, kc, vc)
        n_new = sc_prev[..., None] * n + jnp.einsum('bsh,bshk->bhk', sc, kc)
        return (C_new, n_new, m_new), h

    xs = tuple(to_chunks(a.astype(f32), L) for a in (q, k, v, ig, lf))
    (C, n, m), hs = lax.scan(step, (C0.astype(f32), n0.astype(f32), m0.astype(f32)), xs)
    return from_chunks(hs), C, n, m


def gla_scan(q, k, v, la, S0):
    T = q.shape[1]
    L = math.gcd(T, GLA_CHUNK)
    causal = jnp.tril(jnp.ones((L, L), dtype=bool))
    f32 = jnp.float32

    def step(S, xs):
        qc, kc, vc, lc = xs
        b = jnp.cumsum(lc, axis=1)
        diff = b[:, :, None] - b[:, None, :]
        decay = jnp.exp(jnp.where(causal[None, :, :, None, None], diff, -jnp.inf))
        att = jnp.einsum('bthk,bshk,btshk->btsh', qc, kc, decay)
        o = jnp.einsum('btsh,bshv->bthv', att, vc) + jnp.einsum('bthk,bhkv->bthv', qc * jnp.exp(b), S)
        bL = b[:, -1]
        S_new = jnp.exp(bL)[..., None] * S + jnp.einsum('bshk,bshv->bhkv', kc * jnp.exp(bL[:, None] - b), vc)
        return S_new, o

    xs = tuple(to_chunks(a.astype(f32), L) for a in (q, k, v, la))
    S, os_ = lax.scan(step, S0.astype(f32), xs)
    return from_chunks(os_), S


def mlstm_mixer(x, conv_buf, C, n, m, w_in, conv_w, conv_b, b_i, b_f, hnorm_w, w_out):
    B, T, _ = x.shape
    proj = x @ w_in
    s0 = 2 * QK_W
    qk_pre, v, o, ig, fg = jnp.split(proj, [s0, s0 + V_W, s0 + 2 * V_W, s0 + 2 * V_W + N_HEADS], axis=-1)
    qk, new_buf = causal_conv(qk_pre, conv_buf, conv_w, conv_b)
    q, k = jnp.split(jax.nn.silu(qk), 2, axis=-1)
    q = q.reshape(B, T, N_HEADS, DK)
    k = k.reshape(B, T, N_HEADS, DK) * (DK ** -0.5)
    v = v.reshape(B, T, N_HEADS, DV)
    ig = (ig + b_i).astype(jnp.float32)
    lf = jax.nn.log_sigmoid((fg + b_f).astype(jnp.float32))
    h, C_new, n_new, m_new = mlstm_scan(q, k, v, ig, lf, C, n, m)
    h = rmsnorm(h, hnorm_w.reshape(N_HEADS, DV)).reshape(B, T, V_W)
    y = ((h * jax.nn.sigmoid(o.astype(jnp.float32))).astype(x.dtype)) @ w_out
    return (y.astype(x.dtype), new_buf.astype(conv_buf.dtype), C_new.astype(C.dtype),
            n_new.astype(n.dtype), m_new.astype(m.dtype))


def gla_mixer(x, S, w_in, w_a2, b_a, hnorm_w, w_out):
    B, T, _ = x.shape
    proj = x @ w_in
    q, k, v, g, a_low = jnp.split(proj, [QK_W, 2 * QK_W, 2 * QK_W + V_W, 2 * QK_W + 2 * V_W], axis=-1)
    q = q.reshape(B, T, N_HEADS, DK) * (DK ** -0.5)
    k = k.reshape(B, T, N_HEADS, DK)
    v = v.reshape(B, T, N_HEADS, DV)
    la = jax.nn.log_sigmoid((a_low @ w_a2 + b_a).astype(jnp.float32)) / GLA_NORMALIZER
    la = la.reshape(B, T, N_HEADS, DK)
    o, S_new = gla_scan(q, k, v, la, S)
    o = rmsnorm(o, hnorm_w.reshape(N_HEADS, DV)).reshape(B, T, V_W)
    y = ((o * jax.nn.silu(g.astype(jnp.float32))).astype(x.dtype)) @ w_out
    return y.astype(x.dtype), S_new.astype(S.dtype)


def trunk(x, C_all, n_all, m_all, conv_all, S_all, norm_w, final_norm_w, ffn_w_gate, ffn_w_up, ffn_w_down,
          mlstm_w_in, mlstm_conv_w, mlstm_conv_b, mlstm_b_i, mlstm_b_f, mlstm_hnorm_w, mlstm_w_out,
          gla_w_in, gla_w_a2, gla_b_a, gla_hnorm_w, gla_w_out):
    new_C, new_n, new_m, new_conv, new_S = [], [], [], [], []
    for l in range(DEPTH):
        x = x + 0.5 * swiglu(rmsnorm(x, norm_w[l, 0]), ffn_w_gate[l, 0], ffn_w_up[l, 0], ffn_w_down[l, 0])
        h = rmsnorm(x, norm_w[l, 1])
        j = l // N_MIXERS
        if l % N_MIXERS == 0:
            y, buf, C, n, m = mlstm_mixer(h, conv_all[j], C_all[j], n_all[j], m_all[j], mlstm_w_in[j],
                                          mlstm_conv_w[j], mlstm_conv_b[j], mlstm_b_i[j], mlstm_b_f[j],
                                          mlstm_hnorm_w[j], mlstm_w_out[j])
            new_C.append(C); new_n.append(n); new_m.append(m); new_conv.append(buf)
        else:
            y, S = gla_mixer(h, S_all[j], gla_w_in[j], gla_w_a2[j], gla_b_a[j], gla_hnorm_w[j], gla_w_out[j])
            new_S.append(S)
        x = x + y
        x = x + 0.5 * swiglu(rmsnorm(x, norm_w[l, 2]), ffn_w_gate[l, 1], ffn_w_up[l, 1], ffn_w_down[l, 1])
    out = rmsnorm(x, final_norm_w)
    return (out, jnp.stack(new_C), jnp.stack(new_n), jnp.stack(new_m), jnp.stack(new_conv), jnp.stack(new_S))


def setup_inputs(seed: int = 0) -> dict:
    key = jax.random.key(seed)
    ks = jax.random.split(key, 24)
    nrm = lambda k, shape, s: jax.random.normal(k, shape, jnp.float32) * s
    mlstm_in_w = 2 * QK_W + 2 * V_W + 2 * N_HEADS
    gla_in_w = 2 * QK_W + 2 * V_W + GATE_RANK
    f_bias = jnp.linspace(3.0, 6.0, N_HEADS, dtype=jnp.float32)[None, :] + nrm(ks[15], (N_MLSTM, N_HEADS), 0.01)
    return {
        'x_prompt': nrm(ks[0], (BATCH, SEQ, D_MODEL), 1.0),
        'x_sample': nrm(ks[1], (DEC_BATCH, DEC_SEQ, D_MODEL), 1.0),
        'state_mlstm_C': nrm(ks[2], (N_MLSTM, DEC_BATCH, N_HEADS, DK, DV), 0.5),
        'state_mlstm_n': nrm(ks[3], (N_MLSTM, DEC_BATCH, N_HEADS, DK), 0.5),
        'state_mlstm_m': nrm(ks[4], (N_MLSTM, DEC_BATCH, N_HEADS), 0.5),
        'state_mlstm_conv': nrm(ks[5], (N_MLSTM, DEC_BATCH, CONV_W - 1, 2 * QK_W), 1.0),
        'state_gla_S': nrm(ks[6], (N_GLA, DEC_BATCH, N_HEADS, DK, DV), 0.5),
        'norm_w': 1.0 + nrm(ks[7], (DEPTH, 3, D_MODEL), 0.02),
        'final_norm_w': 1.0 + nrm(ks[8], (D_MODEL,), 0.02),
        'ffn_w_gate': nrm(ks[9], (DEPTH, 2, D_MODEL, D_FF), D_MODEL ** -0.5),
        'ffn_w_up': nrm(ks[10], (DEPTH, 2, D_MODEL, D_FF), D_MODEL ** -0.5),
        'ffn_w_down': nrm(ks[11], (DEPTH, 2, D_FF, D_MODEL), D_FF ** -0.5),
        'mlstm_w_in': nrm(ks[12], (N_MLSTM, D_MODEL, mlstm_in_w), D_MODEL ** -0.5),
        'mlstm_conv_w': nrm(ks[13], (N_MLSTM, CONV_W, 2 * QK_W), CONV_W ** -0.5),
        'mlstm_conv_b': nrm(ks[14], (N_MLSTM, 2 * QK_W), 0.01),
        'mlstm_b_i': nrm(ks[16], (N_MLSTM, N_HEADS), 0.1),
        'mlstm_b_f': f_bias,
        'mlstm_hnorm_w': 1.0 + nrm(ks[17], (N_MLSTM, V_W), 0.02),
        'mlstm_w_out': nrm(ks[18], (N_MLSTM, V_W, D_MODEL), V_W ** -0.5),
        'gla_w_in': nrm(ks[19], (N_GLA, D_MODEL, gla_in_w), D_MODEL ** -0.5),
        'gla_w_a2': nrm(ks[20], (N_GLA, GATE_RANK, QK_W), GATE_RANK ** -0.5),
        'gla_b_a': nrm(ks[21], (N_GLA, QK_W), 0.01),
        'gla_hnorm_w': 1.0 + nrm(ks[22], (N_GLA, V_W), 0.02),
        'gla_w_out': nrm(ks[23], (N_GLA, V_W, D_MODEL), V_W ** -0.5),
    }


def reference(x_prompt, x_sample, state_mlstm_C, state_mlstm_n, state_mlstm_m, state_mlstm_conv, state_gla_S,
              norm_w, final_norm_w, ffn_w_gate, ffn_w_up, ffn_w_down,
              mlstm_w_in, mlstm_conv_w, mlstm_conv_b, mlstm_b_i, mlstm_b_f, mlstm_hnorm_w, mlstm_w_out,
              gla_w_in, gla_w_a2, gla_b_a, gla_hnorm_w, gla_w_out):
    B = x_prompt.shape[0]
    C0 = jnp.zeros((N_MLSTM, B, N_HEADS, DK, DV), state_mlstm_C.dtype)
    n0 = jnp.zeros((N_MLSTM, B, N_HEADS, DK), state_mlstm_n.dtype)
    m0 = jnp.zeros((N_MLSTM, B, N_HEADS), state_mlstm_m.dtype)
    conv0 = jnp.zeros((N_MLSTM, B, CONV_W - 1, 2 * QK_W), state_mlstm_conv.dtype)
    S0 = jnp.zeros((N_GLA, B, N_HEADS, DK, DV), state_gla_S.dtype)
    y_prompt, C_p, n_p, m_p, conv_p, S_p = trunk(
        x_prompt, C0, n0, m0, conv0, S0, norm_w, final_norm_w, ffn_w_gate, ffn_w_up, ffn_w_down,
        mlstm_w_in, mlstm_conv_w, mlstm_conv_b, mlstm_b_i, mlstm_b_f, mlstm_hnorm_w, mlstm_w_out,
        gla_w_in, gla_w_a2, gla_b_a, gla_hnorm_w, gla_w_out)
    y_sample, C_s, n_s, m_s, conv_s, S_s = trunk(
        x_sample, state_mlstm_C, state_mlstm_n, state_mlstm_m, state_mlstm_conv, state_gla_S,
        norm_w, final_norm_w, ffn_w_gate, ffn_w_up, ffn_w_down,
        mlstm_w_in, mlstm_conv_w, mlstm_conv_b, mlstm_b_i, mlstm_b_f, mlstm_hnorm_w, mlstm_w_out,
        gla_w_in, gla_w_a2, gla_b_a, gla_hnorm_w, gla_w_out)
    return (y_prompt, y_sample, C_p, n_p, m_p, conv_p, S_p, C_s, n_s, m_s, conv_s, S_s)
```

```python
import functools

import jax
import jax.numpy as jnp
from jax import lax
from jax.experimental import pallas as pl
from jax.experimental.pallas import tpu as pltpu

D_MODEL = 1024
DEPTH = 4
N_HEADS = 4
DK = 128
DV = 256
QK_W = N_HEADS * DK
V_W = N_HEADS * DV
CONV_W = 4
GATE_RANK = 16
GLA_NORMALIZER = 16.0
D_FF = 2816
EPS = 1e-6
PROJ_W = 2 * QK_W + 2 * V_W

LANES = 128
SUBLANES = 8
VMEM_LIMIT_BYTES = 58 * 1024 * 1024

TOKEN_TILE = 256
FFN_COLS = 256
MLSTM_CHUNK = 256
GLA_CHUNK = 64
GLA_SUB = 16

BF16 = jnp.bfloat16
F32 = jnp.float32


def _resident(shape):
    return pl.BlockSpec(shape, lambda *_: (0,) * len(shape), pipeline_mode=pl.Buffered(1))


def _gated_dtype(chunk):
    return BF16 if chunk % (2 * SUBLANES) == 0 else F32


def _rms(x, w):
    return x * lax.rsqrt(jnp.mean(x * x, axis=-1, keepdims=True) + EPS) * w


def _silu(x):
    return x * jax.nn.sigmoid(x)


def _log_sigmoid(x):
    return jnp.minimum(x, 0.0) - jnp.log1p(jnp.exp(-jnp.abs(x)))


def _dot(a, b):
    return jnp.dot(a.astype(BF16), b.astype(BF16), preferred_element_type=F32)


def _dot_nt(a, b):
    return lax.dot_general(a.astype(BF16), b.astype(BF16), (((1,), (1,)), ((), ())), preferred_element_type=F32)


def _dot_tn(a, b):
    return lax.dot_general(a.astype(BF16), b.astype(BF16), (((0,), (0,)), ((), ())), preferred_element_type=F32)


def _cumsum_rows(x):
    n = x.shape[0]
    row = lax.broadcasted_iota(jnp.int32, x.shape, 0)
    shift = 1
    while shift < n:
        x = x + jnp.where(row >= shift, pltpu.roll(x, shift, 0), 0.0)
        shift *= 2
    return x


def _col_to_row(col):
    n = col.shape[0]
    eye = lax.broadcasted_iota(jnp.int32, (n, n), 0) == lax.broadcasted_iota(jnp.int32, (n, n), 1)
    return jnp.sum(jnp.where(eye, col, 0.0), axis=0, keepdims=True)


def _row_to_col(row):
    n = row.shape[1]
    eye = lax.broadcasted_iota(jnp.int32, (n, n), 0) == lax.broadcasted_iota(jnp.int32, (n, n), 1)
    return jnp.sum(jnp.where(eye, row, 0.0), axis=1, keepdims=True)


def _ffn(x, nw, wg_ref, wu_ref, wd_ref, h_ref):
    xn = _rms(x, nw).astype(BF16)
    for c in range(0, D_FF, FFN_COLS):
        g = jnp.dot(xn, wg_ref[:, c:c + FFN_COLS], preferred_element_type=F32)
        u = jnp.dot(xn, wu_ref[:, c:c + FFN_COLS], preferred_element_type=F32)
        h_ref[:, c:c + FFN_COLS] = (_silu(g) * u).astype(BF16)
    return x + 0.5 * jnp.dot(h_ref[...], wd_ref[...], preferred_element_type=F32)


def _pre_kernel(*refs, is_gla):
    if is_gla:
        (x_ref, nwa_ref, wg_ref, wu_ref, wd_ref, nwm_ref, win_ref, wsm_ref, wa2_ref, ba_ref,
         x1_ref, p0_ref, p1_ref, p2_ref, sm_ref, h_ref) = refs
    else:
        (x_ref, nwa_ref, wg_ref, wu_ref, wd_ref, nwm_ref, win_ref, wsm_ref,
         x1_ref, p0_ref, p1_ref, p2_ref, sm_ref, h_ref) = refs
    x1 = _ffn(x_ref[...], nwa_ref[...], wg_ref, wu_ref, wd_ref, h_ref)
    x1_ref[...] = x1
    xn = _rms(x1, nwm_ref[...]).astype(BF16)
    for i, p_ref in enumerate((p0_ref, p1_ref, p2_ref)):
        p_ref[...] = jnp.dot(xn, win_ref[:, i * D_MODEL:(i + 1) * D_MODEL], preferred_element_type=F32)
    small = jnp.dot(xn, wsm_ref[...], preferred_element_type=F32)
    if is_gla:
        a = jnp.dot(small.astype(BF16), wa2_ref[...], preferred_element_type=F32) + ba_ref[...]
        sm_ref[...] = _log_sigmoid(a) * (1.0 / GLA_NORMALIZER)
    else:
        sm_ref[...] = small


def _pre_call(x, lw, is_gla):
    m = x.shape[0]
    tile = lambda w: pl.BlockSpec((TOKEN_TILE, w), lambda i: (i, 0))
    sm_w = QK_W if is_gla else LANES
    in_specs = [tile(D_MODEL), _resident((1, D_MODEL)), _resident((D_MODEL, D_FF)), _resident((D_MODEL, D_FF)),
                _resident((D_FF, D_MODEL)), _resident((1, D_MODEL)), _resident((D_MODEL, PROJ_W)),
                _resident((D_MODEL, LANES))]
    args = [x, lw["nw_a"], lw["wg_a"], lw["wu_a"], lw["wd_a"], lw["nw_m"], lw["w_in"], lw["w_small"]]
    if is_gla:
        in_specs += [_resident((LANES, QK_W)), _resident((1, QK_W))]
        args += [lw["w_a2"], lw["b_a"]]
    return pl.pallas_call(
        functools.partial(_pre_kernel, is_gla=is_gla),
        grid=(m // TOKEN_TILE,),
        in_specs=in_specs,
        out_specs=[tile(D_MODEL), tile(D_MODEL), tile(D_MODEL), tile(D_MODEL), tile(sm_w)],
        out_shape=[jax.ShapeDtypeStruct((m, D_MODEL), F32)] * 4 + [jax.ShapeDtypeStruct((m, sm_w), F32)],
        scratch_shapes=[pltpu.VMEM((TOKEN_TILE, D_FF), BF16)],
        compiler_params=pltpu.CompilerParams(dimension_semantics=("parallel",), vmem_limit_bytes=VMEM_LIMIT_BYTES),
        name="pre_gla" if is_gla else "pre_mlstm",
    )(*args)


def _post_kernel(*refs, final):
    if final:
        x_ref, gated_ref, wout_ref, nwb_ref, wg_ref, wu_ref, wd_ref, fw_ref, o_ref, h_ref = refs
    else:
        x_ref, gated_ref, wout_ref, nwb_ref, wg_ref, wu_ref, wd_ref, o_ref, h_ref = refs
    x2 = x_ref[...] + jnp.dot(gated_ref[...].astype(BF16), wout_ref[...], preferred_element_type=F32)
    x3 = _ffn(x2, nwb_ref[...], wg_ref, wu_ref, wd_ref, h_ref)
    o_ref[...] = _rms(x3, fw_ref[...]) if final else x3


def _post_call(x1, gated, lw, final_w):
    m = x1.shape[0]
    tile = pl.BlockSpec((TOKEN_TILE, D_MODEL), lambda i: (i, 0))
    in_specs = [tile, tile, _resident((V_W, D_MODEL)), _resident((1, D_MODEL)), _resident((D_MODEL, D_FF)),
                _resident((D_MODEL, D_FF)), _resident((D_FF, D_MODEL))]
    args = [x1, gated, lw["w_out"], lw["nw_b"], lw["wg_b"], lw["wu_b"], lw["wd_b"]]
    final = final_w is not None
    if final:
        in_specs.append(_resident((1, D_MODEL)))
        args.append(final_w)
    return pl.pallas_call(
        functools.partial(_post_kernel, final=final),
        grid=(m // TOKEN_TILE,),
        in_specs=in_specs,
        out_specs=tile,
        out_shape=jax.ShapeDtypeStruct((m, D_MODEL), F32),
        scratch_shapes=[pltpu.VMEM((TOKEN_TILE, D_FF), BF16)],
        compiler_params=pltpu.CompilerParams(dimension_semantics=("parallel",), vmem_limit_bytes=VMEM_LIMIT_BYTES),
        name="post_final" if final else "post",
    )(*args)


def _mlstm_kernel(*refs, chunk, has_state):
    if has_state:
        (qk_ref, v_ref, o_ref, gt_ref, cw_ref, cb_ref, gb_ref, hw_ref, conv_in, c_in, n_in, m_in,
         out_ref, conv_out, c_ref, n_ref, m_ref, ext_ref) = refs
    else:
        (qk_ref, v_ref, o_ref, gt_ref, cw_ref, cb_ref, gb_ref, hw_ref,
         out_ref, conv_out, c_ref, n_ref, m_ref, ext_ref) = refs
    step = pl.program_id(1)
    tail = SUBLANES

    @pl.when(step == 0)
    def _():
        if has_state:
            ext_ref[0:tail, :] = jnp.zeros((tail, 2 * QK_W), F32)
            ext_ref[tail - (CONV_W - 1):tail, :] = conv_in[0]
            c_ref[...] = c_in[...]
            n_ref[...] = n_in[...]
            m_ref[...] = m_in[...]
        else:
            ext_ref[0:tail, :] = jnp.zeros((tail, 2 * QK_W), F32)
            c_ref[...] = jnp.zeros(c_ref.shape, F32)
            n_ref[...] = jnp.zeros(n_ref.shape, F32)
            m_ref[...] = jnp.zeros(m_ref.shape, F32)

    ext_ref[tail:tail + chunk, :] = qk_ref[...]
    acc = cb_ref[...] + cw_ref[CONV_W - 1:CONV_W, :] * ext_ref[tail:tail + chunk, :]
    for j in range(1, CONV_W):
        acc = acc + cw_ref[CONV_W - 1 - j:CONV_W - j, :] * ext_ref[tail - j:tail - j + chunk, :]
    conv_out[0] = ext_ref[tail + chunk - (CONV_W - 1):tail + chunk, :]
    ext_ref[0:tail, :] = ext_ref[chunk:chunk + tail, :]
    qk = _silu(acc)

    gt = gt_ref[...] + gb_ref[...]
    bcum = _cumsum_rows(_log_sigmoid(gt))
    row = lax.broadcasted_iota(jnp.int32, (chunk, chunk), 0)
    col = lax.broadcasted_iota(jnp.int32, (chunk, chunk), 1)
    causal = col <= row

    for h in range(N_HEADS):
        q = qk[:, h * DK:(h + 1) * DK]
        k = qk[:, QK_W + h * DK:QK_W + (h + 1) * DK] * (DK ** -0.5)
        v = v_ref[:, h * DV:(h + 1) * DV]
        ic = gt[:, h:h + 1]
        b = bcum[:, N_HEADS + h:N_HEADS + h + 1]
        m_prev = m_ref[0, :, h:h + 1]
        c_prev = c_ref[0, h]
        n_prev = n_ref[0, h]

        a_row = _col_to_row(ic - b)
        dm = jnp.where(causal, b + a_row, -jnp.inf)
        mt = jnp.maximum(b + m_prev, jnp.max(dm, axis=1, keepdims=True))
        w_inter = jnp.exp(b + m_prev - mt)
        s = _dot_nt(q, k) * jnp.exp(dm - mt)
        num = _dot(s, v) + w_inter * _dot(q, c_prev)
        den = jnp.sum(s, axis=1, keepdims=True) + w_inter * jnp.sum(q * n_prev, axis=1, keepdims=True)
        hh = num / jnp.maximum(jnp.abs(den), jnp.exp(-mt))

        b_last = b[chunk - 1:chunk, :]
        g = b_last - b + ic
        m_new = jnp.maximum(b_last + m_prev, jnp.max(g, axis=0, keepdims=True))
        sc_prev = jnp.exp(b_last + m_prev - m_new)
        kw = k * jnp.exp(g - m_new)
        c_ref[0, h] = sc_prev * c_prev + _dot_tn(kw, v)
        n_ref[0, h] = sc_prev * n_prev + jnp.sum(kw, axis=0, keepdims=True)
        m_ref[0, :, h:h + 1] = m_new

        hn = _rms(hh, hw_ref[:, h * DV:(h + 1) * DV])
        out_ref[:, h * DV:(h + 1) * DV] = (hn * jax.nn.sigmoid(o_ref[:, h * DV:(h + 1) * DV])).astype(out_ref.dtype)


def _mlstm_call(qk, v, o, gates, lw, batch, seq, state):
    chunk = min(MLSTM_CHUNK, seq)
    nc = seq // chunk
    tile = lambda w: pl.BlockSpec((chunk, w), lambda b, c: (b * nc + c, 0))
    per_b = lambda shape: pl.BlockSpec((1,) + shape, lambda b, c: (b,) + (0,) * len(shape))
    in_specs = [tile(2 * QK_W), tile(V_W), tile(V_W), tile(LANES), _resident((CONV_W, 2 * QK_W)),
                _resident((1, 2 * QK_W)), _resident((1, LANES)), _resident((1, V_W))]
    args = [qk, v, o, gates, lw["conv_w"], lw["conv_b"], lw["gate_b"], lw["hnorm_w"]]
    state_specs = [per_b((CONV_W - 1, 2 * QK_W)), per_b((N_HEADS, DK, DV)), per_b((N_HEADS, 1, DK)),
                   per_b((1, N_HEADS))]
    if state is not None:
        conv, c0, n0, m0 = state
        in_specs += state_specs
        args += [conv, c0, n0.reshape(batch, N_HEADS, 1, DK), m0.reshape(batch, 1, N_HEADS)]
    out, conv_new, c_new, n_new, m_new = pl.pallas_call(
        functools.partial(_mlstm_kernel, chunk=chunk, has_state=state is not None),
        grid=(batch, nc),
        in_specs=in_specs,
        out_specs=[tile(V_W)] + state_specs,
        out_shape=[jax.ShapeDtypeStruct((batch * seq, V_W), _gated_dtype(chunk)),
                   jax.ShapeDtypeStruct((batch, CONV_W - 1, 2 * QK_W), F32),
                   jax.ShapeDtypeStruct((batch, N_HEADS, DK, DV), F32),
                   jax.ShapeDtypeStruct((batch, N_HEADS, 1, DK), F32),
                   jax.ShapeDtypeStruct((batch, 1, N_HEADS), F32)],
        scratch_shapes=[pltpu.VMEM((chunk + SUBLANES, 2 * QK_W), F32)],
        compiler_params=pltpu.CompilerParams(dimension_semantics=("parallel", "arbitrary"),
                                             vmem_limit_bytes=VMEM_LIMIT_BYTES),
        name="mlstm_scan",
    )(*args)
    return out, conv_new, c_new, n_new.reshape(batch, N_HEADS, DK), m_new.reshape(batch, N_HEADS)


def _gla_kernel(*refs, chunk, sub, has_state):
    if has_state:
        qk_ref, v_ref, g_ref, la_ref, hw_ref, s_in, out_ref, s_ref, acc_ref = refs
    else:
        qk_ref, v_ref, g_ref, la_ref, hw_ref, out_ref, s_ref, acc_ref = refs
    step = pl.program_id(1)

    @pl.when(step == 0)
    def _():
        if has_state:
            s_ref[...] = s_in[...]
        else:
            s_ref[...] = jnp.zeros(s_ref.shape, F32)

    bcum = _cumsum_rows(la_ref[...])
    nb = chunk // sub
    sub_row = lax.broadcasted_iota(jnp.int32, (sub, DK), 0)

    for h in range(N_HEADS):
        q = qk_ref[:, h * DK:(h + 1) * DK] * (DK ** -0.5)
        k = qk_ref[:, QK_W + h * DK:QK_W + (h + 1) * DK]
        v = v_ref[:, h * DV:(h + 1) * DV]
        b = bcum[:, h * DK:(h + 1) * DK]
        s_prev = s_ref[0, h]

        acc_ref[...] = _dot(q * jnp.exp(b), s_prev)
        for j in range(nb - 1):
            lo, hi = j * sub, (j + 1) * sub
            b_end = b[hi - 1:hi, :]
            k_dec = k[lo:hi, :] * jnp.exp(b_end - b[lo:hi, :])
            q_dec = q[hi:, :] * jnp.exp(b[hi:, :] - b_end)
            att = _dot_nt(q_dec, k_dec)
            acc_ref[hi:, :] += _dot(att, v[lo:hi, :])
        for i in range(nb):
            lo, hi = i * sub, (i + 1) * sub
            qi, ki, bi, vi = q[lo:hi, :], k[lo:hi, :], b[lo:hi, :], v[lo:hi, :]
            oi = acc_ref[lo:hi, :]
            for s in range(sub):
                dec = jnp.exp(jnp.where(sub_row >= s, bi - bi[s:s + 1, :], -jnp.inf))
                w = jnp.sum(qi * ki[s:s + 1, :] * dec, axis=1, keepdims=True)
                oi = oi + w * vi[s:s + 1, :]
            acc_ref[lo:hi, :] = oi

        b_last = b[chunk - 1:chunk, :]
        k_dec = k * jnp.exp(b_last - b)
        s_ref[0, h] = _row_to_col(jnp.exp(b_last)) * s_prev + _dot_tn(k_dec, v)

        on = _rms(acc_ref[...], hw_ref[:, h * DV:(h + 1) * DV])
        out_ref[:, h * DV:(h + 1) * DV] = (on * _silu(g_ref[:, h * DV:(h + 1) * DV])).astype(out_ref.dtype)


def _gla_call(qk, v, g, la, lw, batch, seq, state):
    chunk = min(GLA_CHUNK, seq)
    sub = min(GLA_SUB, chunk)
    nc = seq // chunk
    tile = lambda w: pl.BlockSpec((chunk, w), lambda b, c: (b * nc + c, 0))
    s_spec = pl.BlockSpec((1, N_HEADS, DK, DV), lambda b, c: (b, 0, 0, 0))
    in_specs = [tile(2 * QK_W), tile(V_W), tile(V_W), tile(QK_W), _resident((1, V_W))]
    args = [qk, v, g, la, lw["hnorm_w"]]
    if state is not None:
        in_specs.append(s_spec)
        args.append(state)
    return pl.pallas_call(
        functools.partial(_gla_kernel, chunk=chunk, sub=sub, has_state=state is not None),
        grid=(batch, nc),
        in_specs=in_specs,
        out_specs=[tile(V_W), s_spec],
        out_shape=[jax.ShapeDtypeStruct((batch * seq, V_W), _gated_dtype(chunk)),
                   jax.ShapeDtypeStruct((batch, N_HEADS, DK, DV), F32)],
        scratch_shapes=[pltpu.VMEM((chunk, DV), F32)],
        compiler_params=pltpu.CompilerParams(dimension_semantics=("parallel", "arbitrary"),
                                             vmem_limit_bytes=VMEM_LIMIT_BYTES),
        name="gla_scan",
    )(*args)


def _layer_weights(l, norm_w, ffn_w_gate, ffn_w_up, ffn_w_down, mlstm, gla):
    j = l // 2
    lw = {
        "nw_a": norm_w[l, 0][None, :], "nw_m": norm_w[l, 1][None, :], "nw_b": norm_w[l, 2][None, :],
        "wg_a": ffn_w_gate[l, 0].astype(BF16), "wu_a": ffn_w_up[l, 0].astype(BF16),
        "wd_a": ffn_w_down[l, 0].astype(BF16),
        "wg_b": ffn_w_gate[l, 1].astype(BF16), "wu_b": ffn_w_up[l, 1].astype(BF16),
        "wd_b": ffn_w_down[l, 1].astype(BF16),
    }
    if l % 2 == 0:
        w_in, conv_w, conv_b, b_i, b_f, hnorm_w, w_out = (a[j] for a in mlstm)
        n_small = 2 * N_HEADS
        lw.update(conv_w=conv_w, conv_b=conv_b[None, :],
                  gate_b=jnp.pad(jnp.concatenate([b_i, b_f]), (0, LANES - n_small))[None, :])
    else:
        w_in, w_a2, b_a, hnorm_w, w_out = (a[j] for a in gla)
        n_small = GATE_RANK
        lw.update(w_a2=jnp.pad(w_a2, ((0, LANES - GATE_RANK), (0, 0))).astype(BF16), b_a=b_a[None, :])
    lw.update(w_in=w_in[:, :PROJ_W].astype(BF16),
              w_small=jnp.pad(w_in[:, PROJ_W:], ((0, 0), (0, LANES - n_small))).astype(BF16),
              hnorm_w=hnorm_w[None, :], w_out=w_out.astype(BF16))
    return lw


def _trunk(x, weights, final_w, state):
    batch, seq, _ = x.shape
    x = x.reshape(batch * seq, D_MODEL)
    new_c, new_n, new_m, new_conv, new_s = [], [], [], [], []
    for l, lw in enumerate(weights):
        j = l // 2
        is_gla = l % 2 == 1
        x1, p0, p1, p2, sm = _pre_call(x, lw, is_gla)
        if is_gla:
            st = None if state is None else state[4][j]
            gated, s_new = _gla_call(p0, p1, p2, sm, lw, batch, seq, st)
            new_s.append(s_new)
        else:
            st = None if state is None else (state[3][j], state[0][j], state[1][j], state[2][j])
            gated, conv_new, c_new, n_new, m_new = _mlstm_call(p0, p1, p2, sm, lw, batch, seq, st)
            new_c.append(c_new); new_n.append(n_new); new_m.append(m_new); new_conv.append(conv_new)
        x = _post_call(x1, gated, lw, final_w if l == DEPTH - 1 else None)
    return (x.reshape(batch, seq, D_MODEL), jnp.stack(new_c), jnp.stack(new_n), jnp.stack(new_m),
            jnp.stack(new_conv), jnp.stack(new_s))


def kernel(x_prompt, x_sample, state_mlstm_C, state_mlstm_n, state_mlstm_m, state_mlstm_conv, state_gla_S, norm_w, final_norm_w, ffn_w_gate, ffn_w_up, ffn_w_down, mlstm_w_in, mlstm_conv_w, mlstm_conv_b, mlstm_b_i, mlstm_b_f, mlstm_hnorm_w, mlstm_w_out, gla_w_in, gla_w_a2, gla_b_a, gla_hnorm_w, gla_w_out):
    mlstm = (mlstm_w_in, mlstm_conv_w, mlstm_conv_b, mlstm_b_i, mlstm_b_f, mlstm_hnorm_w, mlstm_w_out)
    gla = (gla_w_in, gla_w_a2, gla_b_a, gla_hnorm_w, gla_w_out)
    weights = [_layer_weights(l, norm_w, ffn_w_gate, ffn_w_up, ffn_w_down, mlstm, gla) for l in range(DEPTH)]
    final_w = final_norm_w[None, :]
    y_p, c_p, n_p, m_p, conv_p, s_p = _trunk(x_prompt, weights, final_w, None)
    y_s, c_s, n_s, m_s, conv_s, s_s = _trunk(
        x_sample, weights, final_w, (state_mlstm_C, state_mlstm_n, state_mlstm_m, state_mlstm_conv, state_gla_S))
    return (y_p, y_s, c_p, n_p, m_p, conv_p, s_p, c_s, n_s, m_s, conv_s, s_s)
```

```python
import functools
import math

import jax
import jax.numpy as jnp
from jax import lax
from jax.experimental import pallas as pl
from jax.experimental.pallas import tpu as pltpu

D_MODEL = 1024
DEPTH = 4
N_HEADS = 4
DK = 128
DV = 256
QK_W = N_HEADS * DK
V_W = N_HEADS * DV
CONV_W = 4
GATE_RANK = 16
GLA_NORMALIZER = 16.0
D_FF = 2816
EPS = 1e-6
PROJ_W = 2 * QK_W + 2 * V_W
LOG2E = math.log2(math.e)

LANES = 128
SUBLANES = 8
VMEM_LIMIT_BYTES = 58 * 1024 * 1024

TOKEN_TILE = 256
FFN_COLS = 256
MLSTM_ROWS = 256
GLA_ROWS = 128
PACKED_ROWS = 128

BF16 = jnp.bfloat16
F32 = jnp.float32


def _resident(shape, index=None):
    index = tuple(index or ())
    block = (None,) * len(index) + tuple(shape)
    return pl.BlockSpec(block, lambda *_: index + (0,) * len(shape), pipeline_mode=pl.Buffered(1))


def _rms(x, w):
    return x * lax.rsqrt(jnp.mean(x * x, axis=-1, keepdims=True) + EPS) * w


def _silu(x):
    return x * jax.nn.sigmoid(x)


def _log_sigmoid(x):
    return jnp.minimum(x, 0.0) - jnp.log1p(jnp.exp(-jnp.abs(x)))


def _dot(a, b):
    return jnp.dot(a.astype(BF16), b.astype(BF16), preferred_element_type=F32)


def _dot_nt(a, b):
    return lax.dot_general(a.astype(BF16), b.astype(BF16), (((1,), (1,)), ((), ())), preferred_element_type=F32)


def _dot_tn(a, b):
    return lax.dot_general(a.astype(BF16), b.astype(BF16), (((0,), (0,)), ((), ())), preferred_element_type=F32)


def _seg_cumsum(x, seg):
    pos = lax.broadcasted_iota(jnp.int32, x.shape, 0) & (seg - 1)
    shift = 1
    while shift < seg:
        x = x + jnp.where(pos >= shift, pltpu.roll(x, shift, 0), 0.0)
        shift *= 2
    return x


def _group_row(x, group, idx):
    n, c = x.shape
    if n == group:
        return x[idx:idx + 1, :]
    return jnp.concatenate(
        [jnp.broadcast_to(x[g * group + idx:g * group + idx + 1, :], (group, c)) for g in range(n // group)], axis=0)


def _group_max(x, group):
    n, c = x.shape
    if n == group:
        return jnp.max(x, axis=0, keepdims=True)
    return jnp.concatenate(
        [jnp.broadcast_to(jnp.max(x[g * group:(g + 1) * group, :], axis=0, keepdims=True), (group, c))
         for g in range(n // group)], axis=0)


def _col_to_row(col):
    n = col.shape[0]
    eye = lax.broadcasted_iota(jnp.int32, (n, n), 0) == lax.broadcasted_iota(jnp.int32, (n, n), 1)
    return jnp.sum(jnp.where(eye, col, 0.0), axis=0, keepdims=True)


def _row_to_col(row):
    n = row.shape[1]
    eye = lax.broadcasted_iota(jnp.int32, (n, n), 0) == lax.broadcasted_iota(jnp.int32, (n, n), 1)
    return jnp.sum(jnp.where(eye, row, 0.0), axis=1, keepdims=True)


def _cast_kernel(w_ref, o_ref):
    o_ref[...] = w_ref[...].astype(o_ref.dtype)


def _to_bf16(w):
    n, r, c = w.shape
    spec = pl.BlockSpec((1, r, c), lambda i: (i, 0, 0))
    return pl.pallas_call(
        _cast_kernel, grid=(n,), in_specs=[spec], out_specs=spec,
        out_shape=jax.ShapeDtypeStruct(w.shape, BF16),
        compiler_params=pltpu.CompilerParams(dimension_semantics=("parallel",), vmem_limit_bytes=VMEM_LIMIT_BYTES),
        name="cast_bf16",
    )(w)


def _ffn(x, nw, wg_ref, wu_ref, wd_ref, h_ref):
    xn = _rms(x, nw).astype(BF16)
    for c in range(0, D_FF, FFN_COLS):
        g = jnp.dot(xn, wg_ref[:, c:c + FFN_COLS], preferred_element_type=F32)
        u = jnp.dot(xn, wu_ref[:, c:c + FFN_COLS], preferred_element_type=F32)
        h_ref[:, c:c + FFN_COLS] = (_silu(g) * u).astype(BF16)
    return x + 0.5 * jnp.dot(h_ref[...], wd_ref[...], preferred_element_type=F32)


def _pre_kernel(*refs, is_gla):
    if is_gla:
        (x_ref, nw_ref, wg_ref, wu_ref, wd_ref, win_ref, wsm_ref, wa2_ref, ba_ref,
         x1_ref, qk_ref, v_ref, og_ref, sm_ref, h_ref) = refs
    else:
        (x_ref, nw_ref, wg_ref, wu_ref, wd_ref, win_ref, wsm_ref,
         x1_ref, qk_ref, v_ref, og_ref, sm_ref, h_ref) = refs
    x1 = _ffn(x_ref[...], nw_ref[0:1, :], wg_ref, wu_ref, wd_ref, h_ref)
    x1_ref[...] = x1
    xn = _rms(x1, nw_ref[1:2, :]).astype(BF16)
    for i, p_ref in enumerate((qk_ref, v_ref, og_ref)):
        p_ref[...] = jnp.dot(xn, win_ref[:, i * D_MODEL:(i + 1) * D_MODEL],
                             preferred_element_type=F32).astype(p_ref.dtype)
    small = jnp.dot(xn, wsm_ref[...], preferred_element_type=F32)
    if is_gla:
        a = jnp.dot(small.astype(BF16), wa2_ref[...], preferred_element_type=F32) + ba_ref[...]
        sm_ref[...] = _log_sigmoid(a) * (1.0 / GLA_NORMALIZER)
    else:
        sm_ref[...] = small


def _pre_call(x, w, l):
    m = x.shape[0]
    is_gla = l % 2 == 1
    j = l // 2
    tile = lambda width: pl.BlockSpec((TOKEN_TILE, width), lambda i: (i, 0))
    sm_w = QK_W if is_gla else LANES
    mix = w["gla"] if is_gla else w["mlstm"]
    in_specs = [tile(D_MODEL), _resident((3, D_MODEL), (l,)), _resident((D_MODEL, D_FF), (2 * l,)),
                _resident((D_MODEL, D_FF), (2 * l,)), _resident((D_FF, D_MODEL), (2 * l,)),
                _resident((D_MODEL, PROJ_W), (j,)), _resident((D_MODEL, LANES), (j,))]
    args = [x, w["norm"], w["gate"], w["up"], w["down"], mix["w_in"], mix["w_small"]]
    if is_gla:
        in_specs += [_resident((LANES, QK_W), (j,)), _resident((1, QK_W), (j,))]
        args += [mix["w_a2"], mix["b_a"]]
    return pl.pallas_call(
        functools.partial(_pre_kernel, is_gla=is_gla),
        grid=(m // TOKEN_TILE,),
        in_specs=in_specs,
        out_specs=[tile(D_MODEL), tile(D_MODEL), tile(D_MODEL), tile(D_MODEL), tile(sm_w)],
        out_shape=[jax.ShapeDtypeStruct((m, D_MODEL), F32), jax.ShapeDtypeStruct((m, D_MODEL), F32),
                   jax.ShapeDtypeStruct((m, D_MODEL), BF16), jax.ShapeDtypeStruct((m, D_MODEL), F32),
                   jax.ShapeDtypeStruct((m, sm_w), F32)],
        scratch_shapes=[pltpu.VMEM((TOKEN_TILE, D_FF), BF16)],
        compiler_params=pltpu.CompilerParams(dimension_semantics=("parallel",), vmem_limit_bytes=VMEM_LIMIT_BYTES),
        name="pre_gla" if is_gla else "pre_mlstm",
    )(*args)


def _post_kernel(*refs, final):
    if final:
        x_ref, gated_ref, wout_ref, nw_ref, wg_ref, wu_ref, wd_ref, fw_ref, o_ref, h_ref = refs
    else:
        x_ref, gated_ref, wout_ref, nw_ref, wg_ref, wu_ref, wd_ref, o_ref, h_ref = refs
    x2 = x_ref[...] + jnp.dot(gated_ref[...], wout_ref[...], preferred_element_type=F32)
    x3 = _ffn(x2, nw_ref[2:3, :], wg_ref, wu_ref, wd_ref, h_ref)
    o_ref[...] = _rms(x3, fw_ref[...]) if final else x3


def _post_call(x1, gated, w, l):
    m = x1.shape[0]
    is_gla = l % 2 == 1
    j = l // 2
    final = l == DEPTH - 1
    tile = pl.BlockSpec((TOKEN_TILE, D_MODEL), lambda i: (i, 0))
    mix = w["gla"] if is_gla else w["mlstm"]
    in_specs = [tile, tile, _resident((V_W, D_MODEL), (j,)), _resident((3, D_MODEL), (l,)),
                _resident((D_MODEL, D_FF), (2 * l + 1,)), _resident((D_MODEL, D_FF), (2 * l + 1,)),
                _resident((D_FF, D_MODEL), (2 * l + 1,))]
    args = [x1, gated, mix["w_out"], w["norm"], w["gate"], w["up"], w["down"]]
    if final:
        in_specs.append(_resident((1, D_MODEL)))
        args.append(w["final"])
    return pl.pallas_call(
        functools.partial(_post_kernel, final=final),
        grid=(m // TOKEN_TILE,),
        in_specs=in_specs,
        out_specs=tile,
        out_shape=jax.ShapeDtypeStruct((m, D_MODEL), F32),
        scratch_shapes=[pltpu.VMEM((TOKEN_TILE, D_FF), BF16)],
        compiler_params=pltpu.CompilerParams(dimension_semantics=("parallel",), vmem_limit_bytes=VMEM_LIMIT_BYTES),
        name="post_final" if final else "post",
    )(*args)


def _mlstm_kernel(*refs, rows, seg, carry):
    nseg = rows // seg
    if carry:
        (qk_ref, v_ref, o_ref, gt_ref, cw_ref, cb_ref, gb_ref, hw_ref,
         out_ref, conv_out, c_ref, n_ref, m_ref, ext_ref) = refs
    else:
        (qk_ref, v_ref, o_ref, gt_ref, cw_ref, cb_ref, gb_ref, hw_ref, conv_in, c_in, nrep_ref, mrep_ref,
         out_ref, conv_out, c_ref, n_ref, m_ref, ext_ref) = refs
    hist = SUBLANES

    if carry:
        @pl.when(pl.program_id(1) == 0)
        def _():
            ext_ref[0, 0:hist, :] = jnp.zeros((hist, 2 * QK_W), F32)
            c_ref[...] = jnp.zeros(c_ref.shape, F32)
            n_ref[...] = jnp.zeros(n_ref.shape, F32)
            m_ref[...] = jnp.zeros(m_ref.shape, F32)

    pieces = []
    for i in range(nseg):
        if not carry:
            ext_ref[i, hist - (CONV_W - 1):hist, :] = conv_in[i]
        ext_ref[i, hist:hist + seg, :] = qk_ref[i * seg:(i + 1) * seg, :]
        acc = cb_ref[...] + cw_ref[CONV_W - 1:CONV_W, :] * ext_ref[i, hist:hist + seg, :]
        for j in range(1, CONV_W):
            acc = acc + cw_ref[CONV_W - 1 - j:CONV_W - j, :] * ext_ref[i, hist - j:hist - j + seg, :]
        conv_out[i] = ext_ref[i, hist + seg - (CONV_W - 1):hist + seg, :]
        pieces.append(acc)
    if carry:
        ext_ref[0, 0:hist, :] = ext_ref[0, seg:seg + hist, :]
    qk = _silu(pieces[0] if nseg == 1 else jnp.concatenate(pieces, axis=0))

    gt = gt_ref[...] + gb_ref[...]
    bcum = _seg_cumsum(_log_sigmoid(gt), seg)
    row = lax.broadcasted_iota(jnp.int32, (rows, rows), 0)
    col = lax.broadcasted_iota(jnp.int32, (rows, rows), 1)
    shift = seg.bit_length() - 1
    causal = (col <= row) & ((row >> shift) == (col >> shift))

    for h in range(N_HEADS):
        q = qk[:, h * DK:(h + 1) * DK]
        k = qk[:, QK_W + h * DK:QK_W + (h + 1) * DK] * (DK ** -0.5)
        v = v_ref[:, h * DV:(h + 1) * DV]
        ic = gt[:, h:h + 1]
        b = bcum[:, N_HEADS + h:N_HEADS + h + 1]
        if carry:
            m_prev = m_ref[0, :, h:h + 1]
            n_prev = n_ref[0, :, h * DK:(h + 1) * DK]
        else:
            m_prev = mrep_ref[:, h:h + 1]
            n_prev = nrep_ref[:, h * DK:(h + 1) * DK]
        c_prev = [c_ref[0, h] if carry else c_in[i, h] for i in range(nseg)]

        a_row = _col_to_row(ic - b)
        dm = jnp.where(causal, b + a_row, -jnp.inf)
        mt = jnp.maximum(b + m_prev, jnp.max(dm, axis=1, keepdims=True))
        w_inter = jnp.exp(b + m_prev - mt)
        s = _dot_nt(q, k) * jnp.exp(dm - mt)
        inter = [_dot(q[i * seg:(i + 1) * seg, :], c_prev[i]) for i in range(nseg)]
        inter = inter[0] if nseg == 1 else jnp.concatenate(inter, axis=0)
        num = _dot(s, v) + w_inter * inter
        den = jnp.sum(s, axis=1, keepdims=True) + w_inter * jnp.sum(q * n_prev, axis=1, keepdims=True)
        hh = num / jnp.maximum(jnp.abs(den), jnp.exp(-mt))

        b_last = _group_row(b, seg, seg - 1)
        g = b_last - b + ic
        m_new = jnp.maximum(b_last + m_prev, _group_max(g, seg))
        sc_prev = jnp.exp(b_last + m_prev - m_new)
        kw = k * jnp.exp(g - m_new)
        for i in range(nseg):
            lo, hi = i * seg, (i + 1) * seg
            sc_i = sc_prev[lo:lo + 1, :]
            c_ref[i, h] = sc_i * c_prev[i] + _dot_tn(kw[lo:hi, :], v[lo:hi, :])
            n_ref[i, :, h * DK:(h + 1) * DK] = sc_i * n_prev[lo:lo + 1, :] + jnp.sum(kw[lo:hi, :], axis=0, keepdims=True)
            m_ref[i, :, h:h + 1] = m_new[lo:lo + 1, :]

        hn = _rms(hh, hw_ref[:, h * DV:(h + 1) * DV])
        out_ref[:, h * DV:(h + 1) * DV] = (hn * jax.nn.sigmoid(o_ref[:, h * DV:(h + 1) * DV])).astype(out_ref.dtype)


def _mlstm_call(qk, v, o, gates, w, j, batch, seq, state):
    carry = state is None
    if carry:
        rows = seg = min(MLSTM_ROWS, seq)
        grid = (batch, seq // rows)
        nc = seq // rows
        tile = lambda width: pl.BlockSpec((rows, width), lambda b, c: (b * nc + c, 0))
    else:
        rows, seg = PACKED_ROWS, seq
        grid = (batch * seq // rows, 1)
        tile = lambda width: pl.BlockSpec((rows, width), lambda b, c: (b, 0))
    nseg = rows // seg
    per_seq = lambda shape: pl.BlockSpec((nseg,) + shape, lambda b, c: (b,) + (0,) * len(shape))
    mw = w["mlstm"]
    in_specs = [tile(2 * QK_W), tile(V_W), tile(V_W), tile(LANES), _resident((CONV_W, 2 * QK_W), (j,)),
                _resident((1, 2 * QK_W), (j,)), _resident((1, LANES), (j,)), _resident((1, V_W), (j,))]
    args = [qk, v, o, gates, mw["conv_w"], mw["conv_b"], mw["gate_b"], mw["hnorm_w"]]
    state_specs = [per_seq((CONV_W - 1, 2 * QK_W)), per_seq((N_HEADS, DK, DV)), per_seq((1, QK_W)),
                   per_seq((1, N_HEADS))]
    if not carry:
        conv, c0, n0, m0 = state
        in_specs += [state_specs[0], state_specs[1], tile(QK_W), tile(N_HEADS)]
        args += [conv, c0, jnp.repeat(n0.reshape(batch, QK_W), seq, axis=0), jnp.repeat(m0, seq, axis=0)]
    out, conv_new, c_new, n_new, m_new = pl.pallas_call(
        functools.partial(_mlstm_kernel, rows=rows, seg=seg, carry=carry),
        grid=grid,
        in_specs=in_specs,
        out_specs=[tile(V_W)] + state_specs,
        out_shape=[jax.ShapeDtypeStruct((batch * seq, V_W), BF16),
                   jax.ShapeDtypeStruct((batch, CONV_W - 1, 2 * QK_W), F32),
                   jax.ShapeDtypeStruct((batch, N_HEADS, DK, DV), F32),
                   jax.ShapeDtypeStruct((batch, 1, QK_W), F32),
                   jax.ShapeDtypeStruct((batch, 1, N_HEADS), F32)],
        scratch_shapes=[pltpu.VMEM((nseg, seg + SUBLANES, 2 * QK_W), F32)],
        compiler_params=pltpu.CompilerParams(dimension_semantics=("parallel", "arbitrary"),
                                             vmem_limit_bytes=VMEM_LIMIT_BYTES),
        name="mlstm_scan" if carry else "mlstm_packed",
    )(*args)
    return out, conv_new, c_new, n_new.reshape(batch, N_HEADS, DK), m_new.reshape(batch, N_HEADS)


def _gla_kernel(*refs, rows, seg, carry):
    nseg = rows // seg
    if carry:
        qk_ref, v_ref, g_ref, la_ref, hw_ref, out_ref, s_ref = refs
    else:
        qk_ref, v_ref, g_ref, la_ref, hw_ref, s_in, out_ref, s_ref = refs

    if carry:
        @pl.when(pl.program_id(1) == 0)
        def _():
            s_ref[...] = jnp.zeros(s_ref.shape, F32)

    b2_all = _seg_cumsum(la_ref[...], seg) * LOG2E
    row = lax.broadcasted_iota(jnp.int32, (rows, rows), 0)
    col = lax.broadcasted_iota(jnp.int32, (rows, rows), 1)
    rowk = lax.broadcasted_iota(jnp.int32, (rows, DK), 0)
    sub = min(SUBLANES, seg)
    halves = []
    half = seg // 2
    while half >= sub:
        sh = (2 * half).bit_length() - 1
        pair = ((row >> sh) == (col >> sh)) & ((row & (2 * half - 1)) >= half) & ((col & (2 * half - 1)) < half)
        second = (rowk & (2 * half - 1)) >= half
        halves.append((half, pair, second))
        half //= 2
    base = row & ~(sub - 1)
    dmask = [(col == base + s) & ((row & (sub - 1)) >= s) for s in range(sub)]

    for h in range(N_HEADS):
        q = qk_ref[:, h * DK:(h + 1) * DK] * (DK ** -0.5)
        k = qk_ref[:, QK_W + h * DK:QK_W + (h + 1) * DK]
        v = v_ref[:, h * DV:(h + 1) * DV]
        b2 = b2_all[:, h * DK:(h + 1) * DK]
        s_prev = [s_ref[0, h] if carry else s_in[i, h] for i in range(nseg)]

        qd = q * jnp.exp2(b2)
        inter = [_dot(qd[i * seg:(i + 1) * seg, :], s_prev[i]) for i in range(nseg)]
        inter = inter[0] if nseg == 1 else jnp.concatenate(inter, axis=0)

        att = jnp.zeros((rows, rows), F32)
        for half, pair, second in halves:
            ref = _group_row(b2, 2 * half, half - 1)
            x = jnp.where(second, q, k) * jnp.exp2(-jnp.abs(b2 - ref))
            att = jnp.where(pair, _dot_nt(x, x), att)
        for s in range(sub):
            dec = jnp.exp2(b2 - _group_row(b2, sub, s))
            wgt = jnp.sum(q * _group_row(k, sub, s) * dec, axis=1, keepdims=True)
            att = jnp.where(dmask[s], wgt, att)
        o = inter + _dot(att, v)

        b_last = _group_row(b2, seg, seg - 1)
        k_dec = k * jnp.exp2(b_last - b2)
        for i in range(nseg):
            lo, hi = i * seg, (i + 1) * seg
            decay = _row_to_col(jnp.exp2(b_last[lo:lo + 1, :]))
            s_ref[i, h] = decay * s_prev[i] + _dot_tn(k_dec[lo:hi, :], v[lo:hi, :])

        on = _rms(o, hw_ref[:, h * DV:(h + 1) * DV])
        out_ref[:, h * DV:(h + 1) * DV] = (on * _silu(g_ref[:, h * DV:(h + 1) * DV])).astype(out_ref.dtype)


def _gla_call(qk, v, g, la, w, j, batch, seq, state):
    carry = state is None
    if carry:
        rows = seg = min(GLA_ROWS, seq)
        nc = seq // rows
        grid = (batch, nc)
        tile = lambda width: pl.BlockSpec((rows, width), lambda b, c: (b * nc + c, 0))
    else:
        rows, seg = PACKED_ROWS, seq
        grid = (batch * seq // rows, 1)
        tile = lambda width: pl.BlockSpec((rows, width), lambda b, c: (b, 0))
    nseg = rows // seg
    s_spec = pl.BlockSpec((nseg, N_HEADS, DK, DV), lambda b, c: (b, 0, 0, 0))
    in_specs = [tile(2 * QK_W), tile(V_W), tile(V_W), tile(QK_W), _resident((1, V_W), (j,))]
    args = [qk, v, g, la, w["gla"]["hnorm_w"]]
    if not carry:
        in_specs.append(s_spec)
        args.append(state)
    return pl.pallas_call(
        functools.partial(_gla_kernel, rows=rows, seg=seg, carry=carry),
        grid=grid,
        in_specs=in_specs,
        out_specs=[tile(V_W), s_spec],
        out_shape=[jax.ShapeDtypeStruct((batch * seq, V_W), BF16),
                   jax.ShapeDtypeStruct((batch, N_HEADS, DK, DV), F32)],
        compiler_params=pltpu.CompilerParams(dimension_semantics=("parallel", "arbitrary"),
                                             vmem_limit_bytes=VMEM_LIMIT_BYTES),
        name="gla_scan" if carry else "gla_packed",
    )(*args)


def _prepare_weights(norm_w, final_norm_w, ffn_w_gate, ffn_w_up, ffn_w_down,
                     mlstm_w_in, mlstm_conv_w, mlstm_conv_b, mlstm_b_i, mlstm_b_f, mlstm_hnorm_w, mlstm_w_out,
                     gla_w_in, gla_w_a2, gla_b_a, gla_hnorm_w, gla_w_out):
    def small(w_in, n):
        return jnp.pad(w_in[:, :, PROJ_W:], ((0, 0), (0, 0), (0, LANES - n))).astype(BF16)

    gate_b = jnp.pad(jnp.concatenate([mlstm_b_i, mlstm_b_f], axis=1), ((0, 0), (0, LANES - 2 * N_HEADS)))
    return {
        "norm": norm_w, "final": final_norm_w[None, :],
        "gate": _to_bf16(ffn_w_gate.reshape(2 * DEPTH, D_MODEL, D_FF)),
        "up": _to_bf16(ffn_w_up.reshape(2 * DEPTH, D_MODEL, D_FF)),
        "down": _to_bf16(ffn_w_down.reshape(2 * DEPTH, D_FF, D_MODEL)),
        "mlstm": {
            "w_in": _to_bf16(mlstm_w_in), "w_small": small(mlstm_w_in, 2 * N_HEADS),
            "w_out": _to_bf16(mlstm_w_out), "conv_w": mlstm_conv_w, "conv_b": mlstm_conv_b[:, None, :],
            "gate_b": gate_b[:, None, :], "hnorm_w": mlstm_hnorm_w[:, None, :],
        },
        "gla": {
            "w_in": _to_bf16(gla_w_in), "w_small": small(gla_w_in, GATE_RANK), "w_out": _to_bf16(gla_w_out),
            "w_a2": jnp.pad(gla_w_a2, ((0, 0), (0, LANES - GATE_RANK), (0, 0))).astype(BF16),
            "b_a": gla_b_a[:, None, :], "hnorm_w": gla_hnorm_w[:, None, :],
        },
    }


def _trunk(x, w, state):
    batch, seq, _ = x.shape
    x = x.reshape(batch * seq, D_MODEL)
    new_c, new_n, new_m, new_conv, new_s = [], [], [], [], []
    for l in range(DEPTH):
        j = l // 2
        x1, qk, v, og, sm = _pre_call(x, w, l)
        if l % 2 == 1:
            st = None if state is None else state[4][j]
            gated, s_new = _gla_call(qk, v, og, sm, w, j, batch, seq, st)
            new_s.append(s_new)
        else:
            st = None if state is None else (state[3][j], state[0][j], state[1][j], state[2][j])
            gated, conv_new, c_new, n_new, m_new = _mlstm_call(qk, v, og, sm, w, j, batch, seq, st)
            new_c.append(c_new); new_n.append(n_new); new_m.append(m_new); new_conv.append(conv_new)
        x = _post_call(x1, gated, w, l)
    return (x.reshape(batch, seq, D_MODEL), jnp.stack(new_c), jnp.stack(new_n), jnp.stack(new_m),
            jnp.stack(new_conv), jnp.stack(new_s))


def kernel(x_prompt, x_sample, state_mlstm_C, state_mlstm_n, state_mlstm_m, state_mlstm_conv, state_gla_S, norm_w, final_norm_w, ffn_w_gate, ffn_w_up, ffn_w_down, mlstm_w_in, mlstm_conv_w, mlstm_conv_b, mlstm_b_i, mlstm_b_f, mlstm_hnorm_w, mlstm_w_out, gla_w_in, gla_w_a2, gla_b_a, gla_hnorm_w, gla_w_out):
    w = _prepare_weights(norm_w, final_norm_w, ffn_w_gate, ffn_w_up, ffn_w_down,
                         mlstm_w_in, mlstm_conv_w, mlstm_conv_b, mlstm_b_i, mlstm_b_f, mlstm_hnorm_w, mlstm_w_out,
                         gla_w_in, gla_w_a2, gla_b_a, gla_hnorm_w, gla_w_out)
    y_p, c_p, n_p, m_p, conv_p, s_p = _trunk(x_prompt, w, None)
    y_s, c_s, n_s, m_s, conv_s, s_s = _trunk(
        x_sample, w, (state_mlstm_C, state_mlstm_n, state_mlstm_m, state_mlstm_conv, state_gla_S))
    return (y_p, y_s, c_p, n_p, m_p, conv_p, s_p, c_s, n_s, m_s, conv_s, s_s)
```

```python
import functools
import math

import jax
import jax.numpy as jnp
import numpy as np
from jax import lax
from jax.experimental import pallas as pl
from jax.experimental.pallas import tpu as pltpu

D_MODEL = 1024
DEPTH = 4
N_HEADS = 4
DK = 128
DV = 256
QK_W = N_HEADS * DK
V_W = N_HEADS * DV
CONV_W = 4
GATE_RANK = 16
GLA_NORMALIZER = 16.0
D_FF = 2816
EPS = 1e-6
PROJ_W = 2 * QK_W + 2 * V_W
LOG2E = math.log2(math.e)

LANES = 128
SUBLANES = 8
VMEM_LIMIT_BYTES = 58 * 1024 * 1024

TOKEN_TILE = 256
FFN_COLS = 256
GLA_ROWS = 128
PACKED_ROWS = 128

BF16 = jnp.bfloat16
F32 = jnp.float32


def _resident(shape, index=None):
    index = tuple(index or ())
    block = (None,) * len(index) + tuple(shape)
    return pl.BlockSpec(block, lambda *_: index + (0,) * len(shape), pipeline_mode=pl.Buffered(1))


def _rms(x, w):
    return x * lax.rsqrt(jnp.mean(x * x, axis=-1, keepdims=True) + EPS) * w


def _silu(x):
    return x * jax.nn.sigmoid(x)


def _log_sigmoid(x):
    return jnp.minimum(x, 0.0) - jnp.log1p(jnp.exp(-jnp.abs(x)))


def _dot(a, b):
    return jnp.dot(a.astype(BF16), b.astype(BF16), preferred_element_type=F32)


def _dot_nt(a, b):
    return lax.dot_general(a.astype(BF16), b.astype(BF16), (((1,), (1,)), ((), ())), preferred_element_type=F32)


def _dot_tn(a, b):
    return lax.dot_general(a.astype(BF16), b.astype(BF16), (((0,), (0,)), ((), ())), preferred_element_type=F32)


def _seg_cumsum(x, seg):
    n = x.shape[0]
    row = lax.broadcasted_iota(jnp.int32, (n, n), 0)
    col = lax.broadcasted_iota(jnp.int32, (n, n), 1)
    shift = seg.bit_length() - 1
    tri = jnp.where((col <= row) & ((row >> shift) == (col >> shift)), 1.0, 0.0).astype(BF16)
    hi = x.astype(BF16)
    r1 = x - hi.astype(F32)
    mid = r1.astype(BF16)
    lo = (r1 - mid.astype(F32)).astype(BF16)
    return jnp.dot(jnp.concatenate([tri, tri, tri], axis=1), jnp.concatenate([hi, mid, lo], axis=0),
                   preferred_element_type=F32)


def _group_row(x, group, idx):
    n, c = x.shape
    if n == group:
        return x[idx:idx + 1, :]
    return jnp.concatenate(
        [jnp.broadcast_to(x[g * group + idx:g * group + idx + 1, :], (group, c)) for g in range(n // group)], axis=0)


def _group_max(x, group):
    n, c = x.shape
    if n == group:
        return jnp.max(x, axis=0, keepdims=True)
    return jnp.concatenate(
        [jnp.broadcast_to(jnp.max(x[g * group:(g + 1) * group, :], axis=0, keepdims=True), (group, c))
         for g in range(n // group)], axis=0)


def _col_to_row(col):
    n = col.shape[0]
    eye = lax.broadcasted_iota(jnp.int32, (n, n), 0) == lax.broadcasted_iota(jnp.int32, (n, n), 1)
    return jnp.sum(jnp.where(eye, col, 0.0), axis=0, keepdims=True)


def _row_to_col(row):
    n = row.shape[1]
    eye = lax.broadcasted_iota(jnp.int32, (n, n), 0) == lax.broadcasted_iota(jnp.int32, (n, n), 1)
    return jnp.sum(jnp.where(eye, row, 0.0), axis=1, keepdims=True)


def _keep(valid, new, old_ref_value):
    return new if valid is None else jnp.where(valid, new, old_ref_value)


def _cast_kernel(w_ref, o_ref):
    o_ref[...] = w_ref[...].astype(o_ref.dtype)


def _to_bf16(w):
    n, r, c = w.shape
    spec = pl.BlockSpec((1, r, c), lambda i: (i, 0, 0))
    return pl.pallas_call(
        _cast_kernel, grid=(n,), in_specs=[spec], out_specs=spec,
        out_shape=jax.ShapeDtypeStruct(w.shape, BF16),
        compiler_params=pltpu.CompilerParams(dimension_semantics=("parallel",), vmem_limit_bytes=VMEM_LIMIT_BYTES),
        name="cast_bf16",
    )(w)


def _ffn(x, nw, wg_ref, wu_ref, wd_ref, h_ref):
    xn = _rms(x, nw).astype(BF16)
    for c in range(0, D_FF, FFN_COLS):
        g = jnp.dot(xn, wg_ref[:, c:c + FFN_COLS], preferred_element_type=F32)
        u = jnp.dot(xn, wu_ref[:, c:c + FFN_COLS], preferred_element_type=F32)
        h_ref[:, c:c + FFN_COLS] = (_silu(g) * u).astype(BF16)
    return x + 0.5 * jnp.dot(h_ref[...], wd_ref[...], preferred_element_type=F32)


def _pre_kernel(*refs, is_gla):
    if is_gla:
        (x_ref, nw_ref, wg_ref, wu_ref, wd_ref, win_ref, wsm_ref, wa2_ref, ba_ref,
         x1_ref, qk_ref, v_ref, og_ref, sm_ref, h_ref) = refs
    else:
        (x_ref, nw_ref, wg_ref, wu_ref, wd_ref, win_ref, wsm_ref,
         x1_ref, qk_ref, v_ref, og_ref, sm_ref, h_ref) = refs
    x1 = _ffn(x_ref[...], nw_ref[0:1, :], wg_ref, wu_ref, wd_ref, h_ref)
    x1_ref[...] = x1
    xn = _rms(x1, nw_ref[1:2, :]).astype(BF16)
    for i, p_ref in enumerate((qk_ref, v_ref, og_ref)):
        p_ref[...] = jnp.dot(xn, win_ref[:, i * D_MODEL:(i + 1) * D_MODEL],
                             preferred_element_type=F32).astype(p_ref.dtype)
    small = jnp.dot(xn, wsm_ref[...], preferred_element_type=F32)
    if is_gla:
        a = jnp.dot(small.astype(BF16), wa2_ref[...], preferred_element_type=F32) + ba_ref[...]
        sm_ref[...] = _log_sigmoid(a) * (1.0 / GLA_NORMALIZER)
    else:
        sm_ref[...] = small


def _pre_call(x, w, l):
    m = x.shape[0]
    is_gla = l % 2 == 1
    j = l // 2
    tile = lambda width: pl.BlockSpec((TOKEN_TILE, width), lambda i: (i, 0))
    sm_w = QK_W if is_gla else LANES
    mix = w["gla"] if is_gla else w["mlstm"]
    in_specs = [tile(D_MODEL), _resident((3, D_MODEL), (l,)), _resident((D_MODEL, D_FF), (2 * l,)),
                _resident((D_MODEL, D_FF), (2 * l,)), _resident((D_FF, D_MODEL), (2 * l,)),
                _resident((D_MODEL, PROJ_W), (j,)), _resident((D_MODEL, LANES), (j,))]
    args = [x, w["norm"], w["gate"], w["up"], w["down"], mix["w_in"], mix["w_small"]]
    if is_gla:
        in_specs += [_resident((LANES, QK_W), (j,)), _resident((1, QK_W), (j,))]
        args += [mix["w_a2"], mix["b_a"]]
    return pl.pallas_call(
        functools.partial(_pre_kernel, is_gla=is_gla),
        grid=(m // TOKEN_TILE,),
        in_specs=in_specs,
        out_specs=[tile(D_MODEL), tile(D_MODEL), tile(D_MODEL), tile(D_MODEL), tile(sm_w)],
        out_shape=[jax.ShapeDtypeStruct((m, D_MODEL), F32), jax.ShapeDtypeStruct((m, D_MODEL), F32),
                   jax.ShapeDtypeStruct((m, D_MODEL), BF16), jax.ShapeDtypeStruct((m, D_MODEL), F32),
                   jax.ShapeDtypeStruct((m, sm_w), F32)],
        scratch_shapes=[pltpu.VMEM((TOKEN_TILE, D_FF), BF16)],
        compiler_params=pltpu.CompilerParams(dimension_semantics=("parallel",), vmem_limit_bytes=VMEM_LIMIT_BYTES),
        name="pre_gla" if is_gla else "pre_mlstm",
    )(*args)


def _post(x1, gated, post_refs, h_ref, final):
    wout_ref, nw_ref, wg_ref, wu_ref, wd_ref = post_refs[:5]
    x2 = x1 + jnp.dot(gated, wout_ref[...], preferred_element_type=F32)
    x3 = _ffn(x2, nw_ref[2:3, :], wg_ref, wu_ref, wd_ref, h_ref)
    return _rms(x3, post_refs[5][...]) if final else x3


def _post_specs(w, l):
    is_gla = l % 2 == 1
    j = l // 2
    mix = w["gla"] if is_gla else w["mlstm"]
    specs = [_resident((V_W, D_MODEL), (j,)), _resident((3, D_MODEL), (l,)),
             _resident((D_MODEL, D_FF), (2 * l + 1,)), _resident((D_MODEL, D_FF), (2 * l + 1,)),
             _resident((D_FF, D_MODEL), (2 * l + 1,))]
    args = [mix["w_out"], w["norm"], w["gate"], w["up"], w["down"]]
    if l == DEPTH - 1:
        specs.append(_resident((1, D_MODEL)))
        args.append(w["final"])
    return specs, args


def _post_kernel(*refs, final):
    x_ref, gated_ref = refs[:2]
    o_ref, h_ref = refs[-2:]
    o_ref[...] = _post(x_ref[...], gated_ref[...], refs[2:-2], h_ref, final)


def _post_call(x1, gated, w, l):
    m = x1.shape[0]
    final = l == DEPTH - 1
    tile = pl.BlockSpec((TOKEN_TILE, D_MODEL), lambda i: (i, 0))
    specs, args = _post_specs(w, l)
    return pl.pallas_call(
        functools.partial(_post_kernel, final=final),
        grid=(m // TOKEN_TILE,),
        in_specs=[tile, tile] + specs,
        out_specs=tile,
        out_shape=jax.ShapeDtypeStruct((m, D_MODEL), F32),
        scratch_shapes=[pltpu.VMEM((TOKEN_TILE, D_FF), BF16)],
        compiler_params=pltpu.CompilerParams(dimension_semantics=("parallel",), vmem_limit_bytes=VMEM_LIMIT_BYTES),
        name="post_final" if final else "post",
    )(x1, gated, *args)


def _mlstm_init(first, state_out, ext_ref):
    conv_out, c_ref, n_ref, m_ref = state_out

    @pl.when(first)
    def _():
        ext_ref[0, 0:SUBLANES, :] = jnp.zeros((SUBLANES, 2 * QK_W), F32)
        conv_out[...] = jnp.zeros(conv_out.shape, F32)
        c_ref[...] = jnp.zeros(c_ref.shape, F32)
        n_ref[...] = jnp.zeros(n_ref.shape, F32)
        m_ref[...] = jnp.zeros(m_ref.shape, F32)


def _mlstm_prologue(tile_refs, w_refs, state_in, state_out, ext_ref, *, rows, seg, valid=None):
    qk_ref, gt_ref = tile_refs[0], tile_refs[3]
    cw_ref, cb_ref, gb_ref = w_refs[:3]
    conv_out = state_out[0]
    carry = state_in is None
    nseg = rows // seg
    hist = SUBLANES

    pieces = []
    for i in range(nseg):
        if not carry:
            ext_ref[i, hist - (CONV_W - 1):hist, :] = state_in[0][i]
        ext_ref[i, hist:hist + seg, :] = qk_ref[i * seg:(i + 1) * seg, :]
        acc = cb_ref[...] + cw_ref[CONV_W - 1:CONV_W, :] * ext_ref[i, hist:hist + seg, :]
        for j in range(1, CONV_W):
            acc = acc + cw_ref[CONV_W - 1 - j:CONV_W - j, :] * ext_ref[i, hist - j:hist - j + seg, :]
        conv_out[i] = _keep(valid, ext_ref[i, hist + seg - (CONV_W - 1):hist + seg, :], conv_out[i])
        pieces.append(acc)
    if carry:
        ext_ref[0, 0:hist, :] = ext_ref[0, seg:seg + hist, :]
    qk = _silu(pieces[0] if nseg == 1 else jnp.concatenate(pieces, axis=0))

    gt = gt_ref[...] + gb_ref[...]
    return qk, gt, _seg_cumsum(_log_sigmoid(gt), seg)


def _mlstm_head(h, ctx, tile_refs, w_refs, state_in, state_out, out_ref, *, rows, seg, valid=None):
    qk, gt, bcum = ctx
    v_ref, o_ref = tile_refs[1], tile_refs[2]
    hw_ref = w_refs[3]
    _, c_ref, n_ref, m_ref = state_out
    carry = state_in is None
    nseg = rows // seg
    row = lax.broadcasted_iota(jnp.int32, (rows, rows), 0)
    col = lax.broadcasted_iota(jnp.int32, (rows, rows), 1)
    shift = seg.bit_length() - 1
    causal = (col <= row) & ((row >> shift) == (col >> shift))

    q = qk[:, h * DK:(h + 1) * DK]
    k = qk[:, QK_W + h * DK:QK_W + (h + 1) * DK] * (DK ** -0.5)
    v = v_ref[:, h * DV:(h + 1) * DV]
    ic = gt[:, h:h + 1]
    b = bcum[:, N_HEADS + h:N_HEADS + h + 1]
    if carry:
        m_prev = m_ref[0, :, h:h + 1]
        n_prev = n_ref[0, :, h * DK:(h + 1) * DK]
    else:
        m_prev = state_in[3][:, h:h + 1]
        n_prev = state_in[2][:, h * DK:(h + 1) * DK]
    c_prev = [c_ref[0, h] if carry else state_in[1][i, h] for i in range(nseg)]

    qk_t = _dot_nt(q, k)
    inter = [_dot(q[i * seg:(i + 1) * seg, :], c_prev[i]) for i in range(nseg)]
    inter = inter[0] if nseg == 1 else jnp.concatenate(inter, axis=0)
    yield

    a_row = _col_to_row(ic - b)
    dm = jnp.where(causal, b + a_row, -jnp.inf)
    mt = jnp.maximum(b + m_prev, jnp.max(dm, axis=1, keepdims=True))
    w_inter = jnp.exp(b + m_prev - mt)
    s = qk_t * jnp.exp(dm - mt)
    intra = _dot(s, v)
    b_last = _group_row(b, seg, seg - 1)
    g = b_last - b + ic
    m_new = jnp.maximum(b_last + m_prev, _group_max(g, seg))
    sc_prev = jnp.exp(b_last + m_prev - m_new)
    kw = k * jnp.exp(g - m_new)
    kv = [_dot_tn(kw[i * seg:(i + 1) * seg, :], v[i * seg:(i + 1) * seg, :]) for i in range(nseg)]
    yield

    num = intra + w_inter * inter
    den = jnp.sum(s, axis=1, keepdims=True) + w_inter * jnp.sum(q * n_prev, axis=1, keepdims=True)
    hh = num / jnp.maximum(jnp.abs(den), jnp.exp(-mt))
    for i in range(nseg):
        lo, hi = i * seg, (i + 1) * seg
        sc_i = sc_prev[lo:lo + 1, :]
        c_ref[i, h] = _keep(valid, sc_i * c_prev[i] + kv[i], c_prev[i])
        n_old = n_prev[lo:lo + 1, :]
        n_ref[i, :, h * DK:(h + 1) * DK] = _keep(
            valid, sc_i * n_old + jnp.sum(kw[lo:hi, :], axis=0, keepdims=True), n_old)
        m_ref[i, :, h:h + 1] = _keep(valid, m_new[lo:lo + 1, :], m_prev[lo:lo + 1, :])
    hn = _rms(hh, hw_ref[:, h * DV:(h + 1) * DV])
    out_ref[:, h * DV:(h + 1) * DV] = (hn * jax.nn.sigmoid(o_ref[:, h * DV:(h + 1) * DV])).astype(out_ref.dtype)
    yield


def _mlstm_weight_specs(w, j):
    mw = w["mlstm"]
    specs = [_resident((CONV_W, 2 * QK_W), (j,)), _resident((1, 2 * QK_W), (j,)), _resident((1, LANES), (j,)),
             _resident((1, V_W), (j,))]
    return specs, [mw["conv_w"], mw["conv_b"], mw["gate_b"], mw["hnorm_w"]]


def _mlstm_packed_kernel(*refs, seg, aliased):
    outs = refs[13:] if aliased else refs[12:]
    ctx = _mlstm_prologue(refs[0:4], refs[4:8], refs[8:12], outs[1:5], outs[5], rows=PACKED_ROWS, seg=seg)
    for h in range(N_HEADS):
        for _ in _mlstm_head(h, ctx, refs[0:4], refs[4:8], refs[8:12], outs[1:5], outs[0], rows=PACKED_ROWS, seg=seg):
            pass


def _mlstm_packed_call(qk, v, o, gates, w, j, batch, seq, conv_all, c_all, n0, m0, c_stack):
    rows, seg = PACKED_ROWS, seq
    nseg = rows // seg
    tile = lambda width: pl.BlockSpec((rows, width), lambda b: (b, 0))
    per_seq = lambda shape: pl.BlockSpec((nseg,) + shape, lambda b: (b,) + (0,) * len(shape))
    layer_seq = lambda shape: pl.BlockSpec((None, nseg) + shape, lambda b: (j, b) + (0,) * len(shape))
    w_specs, w_args = _mlstm_weight_specs(w, j)
    in_specs = ([tile(2 * QK_W), tile(V_W), tile(V_W), tile(LANES)] + w_specs
                + [layer_seq((CONV_W - 1, 2 * QK_W)), layer_seq((N_HEADS, DK, DV)), tile(QK_W), tile(N_HEADS)])
    args = [qk, v, o, gates] + w_args + [conv_all, c_all, jnp.repeat(n0.reshape(batch, QK_W), seq, axis=0),
                                         jnp.repeat(m0, seq, axis=0)]
    aliases = {}
    if c_stack is not None:
        in_specs.append(pl.BlockSpec(memory_space=pl.ANY))
        args.append(c_stack)
        aliases = {len(args) - 1: 2}
    out, conv_new, c_stack, n_new, m_new = pl.pallas_call(
        functools.partial(_mlstm_packed_kernel, seg=seg, aliased=bool(aliases)),
        grid=(batch * seq // rows,),
        in_specs=in_specs,
        out_specs=[tile(V_W), per_seq((CONV_W - 1, 2 * QK_W)), layer_seq((N_HEADS, DK, DV)), per_seq((1, QK_W)),
                   per_seq((1, N_HEADS))],
        out_shape=[jax.ShapeDtypeStruct((batch * seq, V_W), BF16),
                   jax.ShapeDtypeStruct((batch, CONV_W - 1, 2 * QK_W), F32),
                   jax.ShapeDtypeStruct(c_all.shape, F32),
                   jax.ShapeDtypeStruct((batch, 1, QK_W), F32),
                   jax.ShapeDtypeStruct((batch, 1, N_HEADS), F32)],
        scratch_shapes=[pltpu.VMEM((nseg, seg + SUBLANES, 2 * QK_W), F32)],
        input_output_aliases=aliases,
        compiler_params=pltpu.CompilerParams(dimension_semantics=("parallel",), vmem_limit_bytes=VMEM_LIMIT_BYTES),
        name="mlstm_packed",
    )(*args)
    return out, conv_new, c_stack, n_new.reshape(batch, N_HEADS, DK), m_new.reshape(batch, N_HEADS)


def _gla_init(first, s_ref):
    @pl.when(first)
    def _():
        s_ref[...] = jnp.zeros(s_ref.shape, F32)


def _gla_halves(seg):
    halves, half = [], seg // 2
    while half >= SUBLANES:
        halves.append(half)
        half //= 2
    return halves


def _gla_codes(rows, seg):
    sub = min(SUBLANES, seg)
    r = np.arange(rows)[:, None]
    c = np.arange(rows)[None, :]
    code = np.full((rows, rows), -1, np.int32)
    halves = _gla_halves(seg)
    for level, half in enumerate(halves):
        grp = 2 * half
        code[(r // grp == c // grp) & (r % grp >= half) & (c % grp < half)] = level
    inside = (r // sub == c // sub) & (c <= r)
    return np.where(inside, len(halves) + (c % sub), code).astype(np.int32)


def _gla_prologue(la_ref, seg):
    return _seg_cumsum(la_ref[...], seg) * LOG2E


def _gla_head(h, b2, tile_refs, hw_ref, code_ref, s_in, s_ref, out_ref, *, rows, seg, valid=None):
    qk_ref, v_ref, g_ref = tile_refs[:3]
    carry = s_in is None
    nseg = rows // seg
    sub = min(SUBLANES, seg)
    halves = _gla_halves(seg)
    code = code_ref[...]
    rowk = lax.broadcasted_iota(jnp.int32, (rows, DK), 0)

    q = qk_ref[:, h * DK:(h + 1) * DK] * (DK ** -0.5)
    k = qk_ref[:, QK_W + h * DK:QK_W + (h + 1) * DK]
    v = v_ref[:, h * DV:(h + 1) * DV]
    s_prev = [s_ref[0, h] if carry else s_in[i, h] for i in range(nseg)]

    qd = q * jnp.exp2(b2)
    inter = [_dot(qd[i * seg:(i + 1) * seg, :], s_prev[i]) for i in range(nseg)]
    inter = inter[0] if nseg == 1 else jnp.concatenate(inter, axis=0)
    cross = []
    for half in halves:
        second = (rowk & (2 * half - 1)) >= half
        ref = _group_row(b2, 2 * half, half - 1)
        x = jnp.where(second, q, k) * jnp.exp2(-jnp.abs(b2 - ref))
        cross.append(_dot_nt(x, x))
    yield

    att = jnp.zeros((rows, rows), F32)
    for level, prod in enumerate(cross):
        att = jnp.where(code == level, prod, att)
    for s in range(sub):
        dec = jnp.exp2(b2 - _group_row(b2, sub, s))
        wgt = jnp.sum(q * _group_row(k, sub, s) * dec, axis=1, keepdims=True)
        att = jnp.where(code == len(halves) + s, wgt, att)
    intra = _dot(att, v)
    b_last = _group_row(b2, seg, seg - 1)
    k_dec = k * jnp.exp2(b_last - b2)
    kv = [_dot_tn(k_dec[i * seg:(i + 1) * seg, :], v[i * seg:(i + 1) * seg, :]) for i in range(nseg)]
    yield

    for i in range(nseg):
        decay = _row_to_col(jnp.exp2(b_last[i * seg:i * seg + 1, :]))
        s_ref[i, h] = _keep(valid, decay * s_prev[i] + kv[i], s_prev[i])
    on = _rms(inter + intra, hw_ref[:, h * DV:(h + 1) * DV])
    out_ref[:, h * DV:(h + 1) * DV] = (on * _silu(g_ref[:, h * DV:(h + 1) * DV])).astype(out_ref.dtype)
    yield


def _gla_packed_kernel(qk_ref, v_ref, g_ref, la_ref, hw_ref, code_ref, s_in, *rest, seg):
    out_ref, s_ref = rest[-2:]
    b2_all = _gla_prologue(la_ref, seg)
    for h in range(N_HEADS):
        for _ in _gla_head(h, b2_all[:, h * DK:(h + 1) * DK], (qk_ref, v_ref, g_ref), hw_ref, code_ref, s_in, s_ref,
                           out_ref, rows=PACKED_ROWS, seg=seg):
            pass


def _gla_packed_call(qk, v, g, la, w, j, batch, seq, s_all, s_stack):
    rows, seg = PACKED_ROWS, seq
    nseg = rows // seg
    tile = lambda width: pl.BlockSpec((rows, width), lambda b: (b, 0))
    s_spec = pl.BlockSpec((None, nseg, N_HEADS, DK, DV), lambda b: (j, b, 0, 0, 0))
    in_specs = [tile(2 * QK_W), tile(V_W), tile(V_W), tile(QK_W), _resident((1, V_W), (j,)),
                _resident((rows, rows)), s_spec]
    args = [qk, v, g, la, w["gla"]["hnorm_w"], _gla_codes(rows, seg), s_all]
    aliases = {}
    if s_stack is not None:
        in_specs.append(pl.BlockSpec(memory_space=pl.ANY))
        args.append(s_stack)
        aliases = {len(args) - 1: 1}
    return pl.pallas_call(
        functools.partial(_gla_packed_kernel, seg=seg),
        grid=(batch * seq // rows,),
        in_specs=in_specs,
        out_specs=[tile(V_W), s_spec],
        out_shape=[jax.ShapeDtypeStruct((batch * seq, V_W), BF16), jax.ShapeDtypeStruct(s_all.shape, F32)],
        input_output_aliases=aliases,
        compiler_params=pltpu.CompilerParams(dimension_semantics=("parallel",), vmem_limit_bytes=VMEM_LIMIT_BYTES),
        name="gla_packed",
    )(*args)


def _scanpost_kernel(*refs, is_gla, rows, tiles_per_seq, ntiles, final):
    n_scan_w = 2 if is_gla else 4
    n_post_w = 6 if final else 5
    n_state = 1 if is_gla else 4
    tile_refs = refs[0:4]
    scan_w = refs[4:4 + n_scan_w]
    x_ref = refs[4 + n_scan_w]
    wout_ref, nw_ref, wg_ref, wu_ref, wd_ref = refs[5 + n_scan_w:10 + n_scan_w]
    o_ref = refs[5 + n_scan_w + n_post_w]
    state_out = refs[6 + n_scan_w + n_post_w:6 + n_scan_w + n_post_w + n_state]
    scratch = refs[6 + n_scan_w + n_post_w + n_state:]
    gated_ref, h_ref = scratch[0], scratch[1]

    step = pl.program_id(0)
    valid = step < ntiles
    first = jnp.logical_and(valid, lax.rem(step, tiles_per_seq) == 0)

    @pl.when(step == 0)
    def _():
        gated_ref[...] = jnp.zeros(gated_ref.shape, gated_ref.dtype)

    if is_gla:
        _gla_init(first, state_out[0])
    else:
        _mlstm_init(first, state_out, scratch[2])

    def scan_parts():
        if is_gla:
            b2_all = _gla_prologue(tile_refs[3], GLA_ROWS)
            yield
            for h in range(N_HEADS):
                for r0 in range(0, rows, GLA_ROWS):
                    part_refs = tuple(r.at[r0:r0 + GLA_ROWS, :] for r in tile_refs[:3])
                    yield from _gla_head(h, b2_all[r0:r0 + GLA_ROWS, h * DK:(h + 1) * DK], part_refs, scan_w[0],
                                         scan_w[1], None, state_out[0], gated_ref.at[r0:r0 + GLA_ROWS, :],
                                         rows=GLA_ROWS, seg=GLA_ROWS, valid=valid)
        else:
            ctx = _mlstm_prologue(tile_refs, scan_w, None, state_out, scratch[2], rows=rows, seg=rows, valid=valid)
            yield
            for h in range(N_HEADS):
                yield from _mlstm_head(h, ctx, tile_refs, scan_w, None, state_out, gated_ref, rows=rows, seg=rows,
                                       valid=valid)

    n_scan = 1 + 3 * N_HEADS * (rows // GLA_ROWS if is_gla else 1)
    post = {}

    def out_proj():
        post["x2"] = x_ref[...] + jnp.dot(gated_ref[...], wout_ref[...], preferred_element_type=F32)
        post["xn"] = _rms(post["x2"], nw_ref[2:3, :]).astype(BF16)

    def ffn_cols(c):
        g = jnp.dot(post["xn"], wg_ref[:, c:c + FFN_COLS], preferred_element_type=F32)
        u = jnp.dot(post["xn"], wu_ref[:, c:c + FFN_COLS], preferred_element_type=F32)
        h_ref[:, c:c + FFN_COLS] = (_silu(g) * u).astype(BF16)

    def down_proj(lo, hi):
        part = jnp.dot(h_ref[:, lo:hi], wd_ref[lo:hi, :], preferred_element_type=F32)
        post["y"] = part if "y" not in post else post["y"] + part

    post_items = [out_proj]
    n_cols = D_FF // FFN_COLS
    for p in range(N_HEADS):
        lo, hi = FFN_COLS * (p * n_cols // N_HEADS), FFN_COLS * ((p + 1) * n_cols // N_HEADS)
        post_items += [functools.partial(ffn_cols, c) for c in range(lo, hi, FFN_COLS)]
        post_items.append(functools.partial(down_proj, lo, hi))

    done = 0
    for i, _ in enumerate(scan_parts()):
        upto = (i + 1) * len(post_items) // n_scan
        for item in post_items[done:upto]:
            item()
        done = upto
    assert done == len(post_items)

    x3 = post["x2"] + 0.5 * post["y"]
    o_ref[...] = _rms(x3, refs[10 + n_scan_w][...]) if final else x3


def _scanpost_call(x1, qk, v, og, sm, w, l, batch, seq):
    is_gla = l % 2 == 1
    j = l // 2
    final = l == DEPTH - 1
    rows = TOKEN_TILE
    tiles_per_seq = seq // rows
    ntiles = batch * tiles_per_seq
    cur = lambda width: pl.BlockSpec((rows, width), lambda s: (jnp.minimum(s, ntiles - 1), 0))
    prev = pl.BlockSpec((rows, D_MODEL), lambda s: (jnp.maximum(s - 1, 0), 0))
    per_seq = lambda shape: pl.BlockSpec(
        (1,) + shape, lambda s: (jnp.minimum(s, ntiles - 1) // tiles_per_seq,) + (0,) * len(shape))
    if is_gla:
        scan_specs = [_resident((1, V_W), (j,)), _resident((GLA_ROWS, GLA_ROWS))]
        scan_args = [w["gla"]["hnorm_w"], _gla_codes(GLA_ROWS, GLA_ROWS)]
        state_specs = [per_seq((N_HEADS, DK, DV))]
        state_shapes = [jax.ShapeDtypeStruct((batch, N_HEADS, DK, DV), F32)]
        scratch = []
        sm_w = QK_W
    else:
        scan_specs, scan_args = _mlstm_weight_specs(w, j)
        state_specs = [per_seq((CONV_W - 1, 2 * QK_W)), per_seq((N_HEADS, DK, DV)), per_seq((1, QK_W)),
                       per_seq((1, N_HEADS))]
        state_shapes = [jax.ShapeDtypeStruct((batch, CONV_W - 1, 2 * QK_W), F32),
                        jax.ShapeDtypeStruct((batch, N_HEADS, DK, DV), F32),
                        jax.ShapeDtypeStruct((batch, 1, QK_W), F32),
                        jax.ShapeDtypeStruct((batch, 1, N_HEADS), F32)]
        scratch = [pltpu.VMEM((1, rows + SUBLANES, 2 * QK_W), F32)]
        sm_w = LANES
    post_specs, post_args = _post_specs(w, l)
    return pl.pallas_call(
        functools.partial(_scanpost_kernel, is_gla=is_gla, rows=rows, tiles_per_seq=tiles_per_seq, ntiles=ntiles,
                          final=final),
        grid=(ntiles + 1,),
        in_specs=[cur(2 * QK_W), cur(V_W), cur(V_W), cur(sm_w)] + scan_specs + [prev] + post_specs,
        out_specs=[prev] + state_specs,
        out_shape=[jax.ShapeDtypeStruct((batch * seq, D_MODEL), F32)] + state_shapes,
        scratch_shapes=[pltpu.VMEM((rows, V_W), BF16), pltpu.VMEM((rows, D_FF), BF16)] + scratch,
        compiler_params=pltpu.CompilerParams(dimension_semantics=("arbitrary",), vmem_limit_bytes=VMEM_LIMIT_BYTES),
        name=("gla" if is_gla else "mlstm") + "_scanpost",
    )(qk, v, og, sm, *scan_args, x1, *post_args)


def _prepare_weights(norm_w, final_norm_w, ffn_w_gate, ffn_w_up, ffn_w_down,
                     mlstm_w_in, mlstm_conv_w, mlstm_conv_b, mlstm_b_i, mlstm_b_f, mlstm_hnorm_w, mlstm_w_out,
                     gla_w_in, gla_w_a2, gla_b_a, gla_hnorm_w, gla_w_out):
    def small(w_in, n):
        return jnp.pad(w_in[:, :, PROJ_W:], ((0, 0), (0, 0), (0, LANES - n))).astype(BF16)

    gate_b = jnp.pad(jnp.concatenate([mlstm_b_i, mlstm_b_f], axis=1), ((0, 0), (0, LANES - 2 * N_HEADS)))
    return {
        "norm": norm_w, "final": final_norm_w[None, :],
        "gate": _to_bf16(ffn_w_gate.reshape(2 * DEPTH, D_MODEL, D_FF)),
        "up": _to_bf16(ffn_w_up.reshape(2 * DEPTH, D_MODEL, D_FF)),
        "down": _to_bf16(ffn_w_down.reshape(2 * DEPTH, D_FF, D_MODEL)),
        "mlstm": {
            "w_in": _to_bf16(mlstm_w_in), "w_small": small(mlstm_w_in, 2 * N_HEADS),
            "w_out": _to_bf16(mlstm_w_out), "conv_w": mlstm_conv_w, "conv_b": mlstm_conv_b[:, None, :],
            "gate_b": gate_b[:, None, :], "hnorm_w": mlstm_hnorm_w[:, None, :],
        },
        "gla": {
            "w_in": _to_bf16(gla_w_in), "w_small": small(gla_w_in, GATE_RANK), "w_out": _to_bf16(gla_w_out),
            "w_a2": jnp.pad(gla_w_a2, ((0, 0), (0, LANES - GATE_RANK), (0, 0))).astype(BF16),
            "b_a": gla_b_a[:, None, :], "hnorm_w": gla_hnorm_w[:, None, :],
        },
    }


def _prompt_trunk(x, w):
    batch, seq, _ = x.shape
    x = x.reshape(batch * seq, D_MODEL)
    new_c, new_n, new_m, new_conv, new_s = [], [], [], [], []
    for l in range(DEPTH):
        x1, qk, v, og, sm = _pre_call(x, w, l)
        if l % 2 == 1:
            x, s_new = _scanpost_call(x1, qk, v, og, sm, w, l, batch, seq)
            new_s.append(s_new)
        else:
            x, conv_new, c_new, n_new, m_new = _scanpost_call(x1, qk, v, og, sm, w, l, batch, seq)
            new_c.append(c_new); new_conv.append(conv_new)
            new_n.append(n_new.reshape(batch, N_HEADS, DK)); new_m.append(m_new.reshape(batch, N_HEADS))
    return (x.reshape(batch, seq, D_MODEL), jnp.stack(new_c), jnp.stack(new_n), jnp.stack(new_m),
            jnp.stack(new_conv), jnp.stack(new_s))


def _sample_trunk(x, w, c_all, n_all, m_all, conv_all, s_all):
    batch, seq, _ = x.shape
    x = x.reshape(batch * seq, D_MODEL)
    new_n, new_m, new_conv = [], [], []
    c_stack = s_stack = None
    for l in range(DEPTH):
        j = l // 2
        x1, qk, v, og, sm = _pre_call(x, w, l)
        if l % 2 == 1:
            gated, s_stack = _gla_packed_call(qk, v, og, sm, w, j, batch, seq, s_all, s_stack)
        else:
            gated, conv_new, c_stack, n_new, m_new = _mlstm_packed_call(
                qk, v, og, sm, w, j, batch, seq, conv_all, c_all, n_all[j], m_all[j], c_stack)
            new_n.append(n_new); new_m.append(m_new); new_conv.append(conv_new)
        x = _post_call(x1, gated, w, l)
    return (x.reshape(batch, seq, D_MODEL), c_stack, jnp.stack(new_n), jnp.stack(new_m), jnp.stack(new_conv),
            s_stack)


def kernel(x_prompt, x_sample, state_mlstm_C, state_mlstm_n, state_mlstm_m, state_mlstm_conv, state_gla_S, norm_w, final_norm_w, ffn_w_gate, ffn_w_up, ffn_w_down, mlstm_w_in, mlstm_conv_w, mlstm_conv_b, mlstm_b_i, mlstm_b_f, mlstm_hnorm_w, mlstm_w_out, gla_w_in, gla_w_a2, gla_b_a, gla_hnorm_w, gla_w_out):
    w = _prepare_weights(norm_w, final_norm_w, ffn_w_gate, ffn_w_up, ffn_w_down,
                         mlstm_w_in, mlstm_conv_w, mlstm_conv_b, mlstm_b_i, mlstm_b_f, mlstm_hnorm_w, mlstm_w_out,
                         gla_w_in, gla_w_a2, gla_b_a, gla_hnorm_w, gla_w_out)
    y_p, c_p, n_p, m_p, conv_p, s_p = _prompt_trunk(x_prompt, w)
    y_s, c_s, n_s, m_s, conv_s, s_s = _sample_trunk(
        x_sample, w, state_mlstm_C, state_mlstm_n, state_mlstm_m, state_mlstm_conv, state_gla_S)
    return (y_p, y_s, c_p, n_p, m_p, conv_p, s_p, c_s, n_s, m_s, conv_s, s_s)
```

```python
import functools
import math

import jax
import jax.numpy as jnp
import numpy as np
from jax import lax
from jax.experimental import pallas as pl
from jax.experimental.pallas import tpu as pltpu

D_MODEL = 1024
DEPTH = 4
N_HEADS = 4
DK = 128
DV = 256
QK_W = N_HEADS * DK
V_W = N_HEADS * DV
CONV_W = 4
GATE_RANK = 16
GLA_NORMALIZER = 16.0
D_FF = 2816
EPS = 1e-6
PROJ_W = 2 * QK_W + 2 * V_W
LOG2E = math.log2(math.e)

LANES = 128
SUBLANES = 8
VMEM_LIMIT_BYTES = 58 * 1024 * 1024

TOKEN_TILE = 256
PRE_TILE = 512
FFN_COLS = 256
GLA_ROWS = 128
PACKED_ROWS = 128

BF16 = jnp.bfloat16
F32 = jnp.float32


def _resident(shape, index=None):
    index = tuple(index or ())
    block = (None,) * len(index) + tuple(shape)
    return pl.BlockSpec(block, lambda *_: index + (0,) * len(shape), pipeline_mode=pl.Buffered(1))


def _rms(x, w):
    return x * lax.rsqrt(jnp.mean(x * x, axis=-1, keepdims=True) + EPS) * w


def _silu(x):
    return x * jax.nn.sigmoid(x)


def _log_sigmoid(x):
    return jnp.minimum(x, 0.0) - jnp.log1p(jnp.exp(-jnp.abs(x)))


def _dot(a, b):
    return jnp.dot(a.astype(BF16), b.astype(BF16), preferred_element_type=F32)


def _dot_nt(a, b):
    return lax.dot_general(a.astype(BF16), b.astype(BF16), (((1,), (1,)), ((), ())), preferred_element_type=F32)


def _dot_tn(a, b):
    return lax.dot_general(a.astype(BF16), b.astype(BF16), (((0,), (0,)), ((), ())), preferred_element_type=F32)


def _seg_cumsum(x, seg):
    n = x.shape[0]
    row = lax.broadcasted_iota(jnp.int32, (n, n), 0)
    col = lax.broadcasted_iota(jnp.int32, (n, n), 1)
    shift = seg.bit_length() - 1
    tri = jnp.where((col <= row) & ((row >> shift) == (col >> shift)), 1.0, 0.0).astype(BF16)
    hi = x.astype(BF16)
    r1 = x - hi.astype(F32)
    mid = r1.astype(BF16)
    lo = (r1 - mid.astype(F32)).astype(BF16)
    return jnp.dot(jnp.concatenate([tri, tri, tri], axis=1), jnp.concatenate([hi, mid, lo], axis=0),
                   preferred_element_type=F32)


def _group_row(x, group, idx):
    n, c = x.shape
    if n == group:
        return x[idx:idx + 1, :]
    return jnp.concatenate(
        [jnp.broadcast_to(x[g * group + idx:g * group + idx + 1, :], (group, c)) for g in range(n // group)], axis=0)


def _group_max(x, group):
    n, c = x.shape
    if n == group:
        return jnp.max(x, axis=0, keepdims=True)
    return jnp.concatenate(
        [jnp.broadcast_to(jnp.max(x[g * group:(g + 1) * group, :], axis=0, keepdims=True), (group, c))
         for g in range(n // group)], axis=0)


def _col_to_row(col):
    n = col.shape[0]
    eye = lax.broadcasted_iota(jnp.int32, (n, n), 0) == lax.broadcasted_iota(jnp.int32, (n, n), 1)
    return jnp.sum(jnp.where(eye, col, 0.0), axis=0, keepdims=True)


def _row_to_col(row):
    n = row.shape[1]
    eye = lax.broadcasted_iota(jnp.int32, (n, n), 0) == lax.broadcasted_iota(jnp.int32, (n, n), 1)
    return jnp.sum(jnp.where(eye, row, 0.0), axis=1, keepdims=True)


def _keep(valid, new, old_ref_value):
    return new if valid is None else jnp.where(valid, new, old_ref_value)


def _cast_kernel(w_ref, o_ref):
    o_ref[...] = w_ref[...].astype(o_ref.dtype)


def _to_bf16(w):
    n, r, c = w.shape
    spec = pl.BlockSpec((1, r, c), lambda i: (i, 0, 0))
    return pl.pallas_call(
        _cast_kernel, grid=(n,), in_specs=[spec], out_specs=spec,
        out_shape=jax.ShapeDtypeStruct(w.shape, BF16),
        compiler_params=pltpu.CompilerParams(dimension_semantics=("parallel",), vmem_limit_bytes=VMEM_LIMIT_BYTES),
        name="cast_bf16",
    )(w)


def _ffn(x, nw, wg_ref, wu_ref, wd_ref, h_ref):
    xn = _rms(x, nw).astype(BF16)
    for c in range(0, D_FF, FFN_COLS):
        g = jnp.dot(xn, wg_ref[:, c:c + FFN_COLS], preferred_element_type=F32)
        u = jnp.dot(xn, wu_ref[:, c:c + FFN_COLS], preferred_element_type=F32)
        h_ref[:, c:c + FFN_COLS] = (_silu(g) * u).astype(BF16)
    return x + 0.5 * jnp.dot(h_ref[...], wd_ref[...], preferred_element_type=F32)


def _pre_kernel(*refs, is_gla):
    if is_gla:
        (x_ref, nw_ref, wg_ref, wu_ref, wd_ref, win_ref, wsm_ref, wa2_ref, ba_ref,
         x1_ref, qk_ref, v_ref, og_ref, sm_ref, h_ref) = refs
    else:
        (x_ref, nw_ref, wg_ref, wu_ref, wd_ref, win_ref, wsm_ref,
         x1_ref, qk_ref, v_ref, og_ref, sm_ref, h_ref) = refs
    x1 = _ffn(x_ref[...], nw_ref[0:1, :], wg_ref, wu_ref, wd_ref, h_ref)
    x1_ref[...] = x1
    xn = _rms(x1, nw_ref[1:2, :]).astype(BF16)
    for i, p_ref in enumerate((qk_ref, v_ref, og_ref)):
        p_ref[...] = jnp.dot(xn, win_ref[:, i * D_MODEL:(i + 1) * D_MODEL],
                             preferred_element_type=F32).astype(p_ref.dtype)
    small = jnp.dot(xn, wsm_ref[...], preferred_element_type=F32)
    if is_gla:
        a = jnp.dot(small.astype(BF16), wa2_ref[...], preferred_element_type=F32) + ba_ref[...]
        sm_ref[...] = _log_sigmoid(a) * (1.0 / GLA_NORMALIZER)
    else:
        sm_ref[...] = small


def _pre_call(x, w, l):
    m = x.shape[0]
    is_gla = l % 2 == 1
    j = l // 2
    tile = lambda width: pl.BlockSpec((PRE_TILE, width), lambda i: (i, 0))
    sm_w = QK_W if is_gla else LANES
    mix = w["gla"] if is_gla else w["mlstm"]
    in_specs = [tile(D_MODEL), _resident((3, D_MODEL), (l,)), _resident((D_MODEL, D_FF), (2 * l,)),
                _resident((D_MODEL, D_FF), (2 * l,)), _resident((D_FF, D_MODEL), (2 * l,)),
                _resident((D_MODEL, PROJ_W), (j,)), _resident((D_MODEL, LANES), (j,))]
    args = [x, w["norm"], w["gate"], w["up"], w["down"], mix["w_in"], mix["w_small"]]
    if is_gla:
        in_specs += [_resident((LANES, QK_W), (j,)), _resident((1, QK_W), (j,))]
        args += [mix["w_a2"], mix["b_a"]]
    return pl.pallas_call(
        functools.partial(_pre_kernel, is_gla=is_gla),
        grid=(m // PRE_TILE,),
        in_specs=in_specs,
        out_specs=[tile(D_MODEL), tile(D_MODEL), tile(D_MODEL), tile(D_MODEL), tile(sm_w)],
        out_shape=[jax.ShapeDtypeStruct((m, D_MODEL), F32), jax.ShapeDtypeStruct((m, D_MODEL), F32),
                   jax.ShapeDtypeStruct((m, D_MODEL), BF16), jax.ShapeDtypeStruct((m, D_MODEL), F32),
                   jax.ShapeDtypeStruct((m, sm_w), F32)],
        scratch_shapes=[pltpu.VMEM((PRE_TILE, D_FF), BF16)],
        compiler_params=pltpu.CompilerParams(dimension_semantics=("parallel",), vmem_limit_bytes=VMEM_LIMIT_BYTES),
        name="pre_gla" if is_gla else "pre_mlstm",
    )(*args)


def _post(x1, gated, post_refs, h_ref, final):
    wout_ref, nw_ref, wg_ref, wu_ref, wd_ref = post_refs[:5]
    x2 = x1 + jnp.dot(gated, wout_ref[...], preferred_element_type=F32)
    x3 = _ffn(x2, nw_ref[2:3, :], wg_ref, wu_ref, wd_ref, h_ref)
    return _rms(x3, post_refs[5][...]) if final else x3


def _post_specs(w, l):
    is_gla = l % 2 == 1
    j = l // 2
    mix = w["gla"] if is_gla else w["mlstm"]
    specs = [_resident((V_W, D_MODEL), (j,)), _resident((3, D_MODEL), (l,)),
             _resident((D_MODEL, D_FF), (2 * l + 1,)), _resident((D_MODEL, D_FF), (2 * l + 1,)),
             _resident((D_FF, D_MODEL), (2 * l + 1,))]
    args = [mix["w_out"], w["norm"], w["gate"], w["up"], w["down"]]
    if l == DEPTH - 1:
        specs.append(_resident((1, D_MODEL)))
        args.append(w["final"])
    return specs, args


def _post_kernel(*refs, final):
    x_ref, gated_ref = refs[:2]
    o_ref, h_ref = refs[-2:]
    o_ref[...] = _post(x_ref[...], gated_ref[...], refs[2:-2], h_ref, final)


def _post_call(x1, gated, w, l):
    m = x1.shape[0]
    final = l == DEPTH - 1
    tile = pl.BlockSpec((TOKEN_TILE, D_MODEL), lambda i: (i, 0))
    specs, args = _post_specs(w, l)
    return pl.pallas_call(
        functools.partial(_post_kernel, final=final),
        grid=(m // TOKEN_TILE,),
        in_specs=[tile, tile] + specs,
        out_specs=tile,
        out_shape=jax.ShapeDtypeStruct((m, D_MODEL), F32),
        scratch_shapes=[pltpu.VMEM((TOKEN_TILE, D_FF), BF16)],
        compiler_params=pltpu.CompilerParams(dimension_semantics=("parallel",), vmem_limit_bytes=VMEM_LIMIT_BYTES),
        name="post_final" if final else "post",
    )(x1, gated, *args)


def _mlstm_init(first, state_out, ext_ref):
    conv_out, c_ref, n_ref, m_ref = state_out

    @pl.when(first)
    def _():
        ext_ref[0, 0:SUBLANES, :] = jnp.zeros((SUBLANES, 2 * QK_W), F32)
        conv_out[...] = jnp.zeros(conv_out.shape, F32)
        c_ref[...] = jnp.zeros(c_ref.shape, F32)
        n_ref[...] = jnp.zeros(n_ref.shape, F32)
        m_ref[...] = jnp.zeros(m_ref.shape, F32)


def _mlstm_prologue(tile_refs, w_refs, state_in, state_out, ext_ref, *, rows, seg, valid=None):
    qk_ref, gt_ref = tile_refs[0], tile_refs[3]
    cw_ref, cb_ref, gb_ref = w_refs[:3]
    conv_out = state_out[0]
    carry = state_in is None
    nseg = rows // seg
    hist = SUBLANES

    pieces = []
    for i in range(nseg):
        if not carry:
            ext_ref[i, hist - (CONV_W - 1):hist, :] = state_in[0][i]
        ext_ref[i, hist:hist + seg, :] = qk_ref[i * seg:(i + 1) * seg, :]
        acc = cb_ref[...] + cw_ref[CONV_W - 1:CONV_W, :] * ext_ref[i, hist:hist + seg, :]
        for j in range(1, CONV_W):
            acc = acc + cw_ref[CONV_W - 1 - j:CONV_W - j, :] * ext_ref[i, hist - j:hist - j + seg, :]
        conv_out[i] = _keep(valid, ext_ref[i, hist + seg - (CONV_W - 1):hist + seg, :], conv_out[i])
        pieces.append(acc)
    if carry:
        ext_ref[0, 0:hist, :] = ext_ref[0, seg:seg + hist, :]
    qk = _silu(pieces[0] if nseg == 1 else jnp.concatenate(pieces, axis=0))

    gt = gt_ref[...] + gb_ref[...]
    k = qk[:, QK_W:] * (DK ** -0.5)
    return qk[:, :QK_W], k, qk[:, :QK_W].astype(BF16), k.astype(BF16), gt, _seg_cumsum(_log_sigmoid(gt), seg)


def _mlstm_head(h, ctx, tile_refs, w_refs, state_in, state_out, out_ref, *, rows, seg, valid=None):
    q_all, k_all, qb_all, kb_all, gt, bcum = ctx
    v_ref, o_ref = tile_refs[1], tile_refs[2]
    hw_ref = w_refs[3]
    _, c_ref, n_ref, m_ref = state_out
    carry = state_in is None
    nseg = rows // seg
    row = lax.broadcasted_iota(jnp.int32, (rows, rows), 0)
    col = lax.broadcasted_iota(jnp.int32, (rows, rows), 1)
    shift = seg.bit_length() - 1
    causal = (col <= row) & ((row >> shift) == (col >> shift))

    q = q_all[:, h * DK:(h + 1) * DK]
    k = k_all[:, h * DK:(h + 1) * DK]
    qb = qb_all[:, h * DK:(h + 1) * DK]
    v = v_ref[:, h * DV:(h + 1) * DV]
    ic = gt[:, h:h + 1]
    b = bcum[:, N_HEADS + h:N_HEADS + h + 1]
    if carry:
        m_prev = m_ref[0, :, h:h + 1]
        n_prev = n_ref[0, :, h * DK:(h + 1) * DK]
    else:
        m_prev = state_in[3][:, h:h + 1]
        n_prev = state_in[2][:, h * DK:(h + 1) * DK]
    c_prev = [c_ref[0, h] if carry else state_in[1][i, h] for i in range(nseg)]

    qk_t = _dot_nt(qb, kb_all[:, h * DK:(h + 1) * DK])
    inter = [_dot(qb[i * seg:(i + 1) * seg, :], c_prev[i]) for i in range(nseg)]
    inter = inter[0] if nseg == 1 else jnp.concatenate(inter, axis=0)
    a_row = _col_to_row(ic - b)
    dm = jnp.where(causal, b + a_row, -jnp.inf)
    mt = jnp.maximum(b + m_prev, jnp.max(dm, axis=1, keepdims=True))
    w_inter = jnp.exp(b + m_prev - mt)
    p = jnp.exp(dm - mt)
    b_last = _group_row(b, seg, seg - 1)
    g = b_last - b + ic
    m_new = jnp.maximum(b_last + m_prev, _group_max(g, seg))
    sc_prev = jnp.exp(b_last + m_prev - m_new)
    kw = k * jnp.exp(g - m_new)
    kw_b = kw.astype(BF16)
    yield

    s = qk_t * p
    s_b = s.astype(BF16)
    yield

    intra = _dot(s_b, v)
    kv = [_dot_tn(kw_b[i * seg:(i + 1) * seg, :], v[i * seg:(i + 1) * seg, :]) for i in range(nseg)]
    yield

    num = intra + w_inter * inter
    den = jnp.sum(s, axis=1, keepdims=True) + w_inter * jnp.sum(q * n_prev, axis=1, keepdims=True)
    hh = num / jnp.maximum(jnp.abs(den), jnp.exp(-mt))
    for i in range(nseg):
        lo, hi = i * seg, (i + 1) * seg
        sc_i = sc_prev[lo:lo + 1, :]
        c_ref[i, h] = _keep(valid, sc_i * c_prev[i] + kv[i], c_prev[i])
        n_old = n_prev[lo:lo + 1, :]
        n_ref[i, :, h * DK:(h + 1) * DK] = _keep(
            valid, sc_i * n_old + jnp.sum(kw[lo:hi, :], axis=0, keepdims=True), n_old)
        m_ref[i, :, h:h + 1] = _keep(valid, m_new[lo:lo + 1, :], m_prev[lo:lo + 1, :])
    hn = _rms(hh, hw_ref[:, h * DV:(h + 1) * DV])
    out_ref[:, h * DV:(h + 1) * DV] = (hn * jax.nn.sigmoid(o_ref[:, h * DV:(h + 1) * DV])).astype(out_ref.dtype)
    yield


def _mlstm_weight_specs(w, j):
    mw = w["mlstm"]
    specs = [_resident((CONV_W, 2 * QK_W), (j,)), _resident((1, 2 * QK_W), (j,)), _resident((1, LANES), (j,)),
             _resident((1, V_W), (j,))]
    return specs, [mw["conv_w"], mw["conv_b"], mw["gate_b"], mw["hnorm_w"]]


def _mlstm_packed_kernel(*refs, seg, aliased):
    outs = refs[13:] if aliased else refs[12:]
    ctx = _mlstm_prologue(refs[0:4], refs[4:8], refs[8:12], outs[1:5], outs[5], rows=PACKED_ROWS, seg=seg)
    for h in range(N_HEADS):
        for _ in _mlstm_head(h, ctx, refs[0:4], refs[4:8], refs[8:12], outs[1:5], outs[0], rows=PACKED_ROWS, seg=seg):
            pass


def _mlstm_packed_call(qk, v, o, gates, w, j, batch, seq, conv_all, c_all, n0, m0, c_stack):
    rows, seg = PACKED_ROWS, seq
    nseg = rows // seg
    tile = lambda width: pl.BlockSpec((rows, width), lambda b: (b, 0))
    per_seq = lambda shape: pl.BlockSpec((nseg,) + shape, lambda b: (b,) + (0,) * len(shape))
    layer_seq = lambda shape: pl.BlockSpec((None, nseg) + shape, lambda b: (j, b) + (0,) * len(shape))
    w_specs, w_args = _mlstm_weight_specs(w, j)
    in_specs = ([tile(2 * QK_W), tile(V_W), tile(V_W), tile(LANES)] + w_specs
                + [layer_seq((CONV_W - 1, 2 * QK_W)), layer_seq((N_HEADS, DK, DV)), tile(QK_W), tile(N_HEADS)])
    args = [qk, v, o, gates] + w_args + [conv_all, c_all, jnp.repeat(n0.reshape(batch, QK_W), seq, axis=0),
                                         jnp.repeat(m0, seq, axis=0)]
    aliases = {}
    if c_stack is not None:
        in_specs.append(pl.BlockSpec(memory_space=pl.ANY))
        args.append(c_stack)
        aliases = {len(args) - 1: 2}
    out, conv_new, c_stack, n_new, m_new = pl.pallas_call(
        functools.partial(_mlstm_packed_kernel, seg=seg, aliased=bool(aliases)),
        grid=(batch * seq // rows,),
        in_specs=in_specs,
        out_specs=[tile(V_W), per_seq((CONV_W - 1, 2 * QK_W)), layer_seq((N_HEADS, DK, DV)), per_seq((1, QK_W)),
                   per_seq((1, N_HEADS))],
        out_shape=[jax.ShapeDtypeStruct((batch * seq, V_W), BF16),
                   jax.ShapeDtypeStruct((batch, CONV_W - 1, 2 * QK_W), F32),
                   jax.ShapeDtypeStruct(c_all.shape, F32),
                   jax.ShapeDtypeStruct((batch, 1, QK_W), F32),
                   jax.ShapeDtypeStruct((batch, 1, N_HEADS), F32)],
        scratch_shapes=[pltpu.VMEM((nseg, seg + SUBLANES, 2 * QK_W), F32)],
        input_output_aliases=aliases,
        compiler_params=pltpu.CompilerParams(dimension_semantics=("parallel",), vmem_limit_bytes=VMEM_LIMIT_BYTES),
        name="mlstm_packed",
    )(*args)
    return out, conv_new, c_stack, n_new.reshape(batch, N_HEADS, DK), m_new.reshape(batch, N_HEADS)


def _gla_init(first, s_ref):
    @pl.when(first)
    def _():
        s_ref[...] = jnp.zeros(s_ref.shape, F32)


def _gla_halves(seg):
    halves, half = [], seg // 2
    while half >= SUBLANES:
        halves.append(half)
        half //= 2
    return halves


def _gla_codes(rows, seg):
    sub = min(SUBLANES, seg)
    r = np.arange(rows)[:, None]
    c = np.arange(rows)[None, :]
    code = np.full((rows, rows), -1, np.int32)
    halves = _gla_halves(seg)
    for level, half in enumerate(halves):
        grp = 2 * half
        code[(r // grp == c // grp) & (r % grp >= half) & (c % grp < half)] = level
    inside = (r // sub == c // sub) & (c <= r)
    return np.where(inside, len(halves) + (c % sub), code).astype(np.int32)


def _gla_prologue(la_ref, seg):
    return _seg_cumsum(la_ref[...], seg) * LOG2E


def _gla_head(h, b2, tile_refs, hw_ref, code_ref, s_in, s_ref, out_ref, *, rows, seg, valid=None):
    qk_ref, v_ref, g_ref = tile_refs[:3]
    carry = s_in is None
    nseg = rows // seg
    sub = min(SUBLANES, seg)
    halves = _gla_halves(seg)
    code = code_ref[...]
    rowk = lax.broadcasted_iota(jnp.int32, (rows, DK), 0)

    q = qk_ref[:, h * DK:(h + 1) * DK] * (DK ** -0.5)
    k = qk_ref[:, QK_W + h * DK:QK_W + (h + 1) * DK]
    v = v_ref[:, h * DV:(h + 1) * DV]
    s_prev = [s_ref[0, h] if carry else s_in[i, h] for i in range(nseg)]

    qd = (q * jnp.exp2(b2)).astype(BF16)
    s_prev_b = [s.astype(BF16) for s in s_prev]
    xs = []
    for half in halves:
        second = (rowk & (2 * half - 1)) >= half
        ref = _group_row(b2, 2 * half, half - 1)
        xs.append((jnp.where(second, q, k) * jnp.exp2(-jnp.abs(b2 - ref))).astype(BF16))
    yield

    inter = [_dot(qd[i * seg:(i + 1) * seg, :], s_prev_b[i]) for i in range(nseg)]
    inter = inter[0] if nseg == 1 else jnp.concatenate(inter, axis=0)
    cross = [_dot_nt(x, x) for x in xs]
    wgts = []
    for s in range(sub):
        dec = jnp.exp2(b2 - _group_row(b2, sub, s))
        wgts.append(jnp.sum(q * _group_row(k, sub, s) * dec, axis=1, keepdims=True))
    b_last = _group_row(b2, seg, seg - 1)
    k_dec = (k * jnp.exp2(b_last - b2)).astype(BF16)
    yield

    att = jnp.zeros((rows, rows), F32)
    for level, prod in enumerate(cross):
        att = jnp.where(code == level, prod, att)
    for s, wgt in enumerate(wgts):
        att = jnp.where(code == len(halves) + s, wgt, att)
    att = att.astype(BF16)
    yield

    intra = _dot(att, v)
    kv = [_dot_tn(k_dec[i * seg:(i + 1) * seg, :], v[i * seg:(i + 1) * seg, :]) for i in range(nseg)]
    yield

    for i in range(nseg):
        decay = _row_to_col(jnp.exp2(b_last[i * seg:i * seg + 1, :]))
        s_ref[i, h] = _keep(valid, decay * s_prev[i] + kv[i], s_prev[i])
    on = _rms(inter + intra, hw_ref[:, h * DV:(h + 1) * DV])
    out_ref[:, h * DV:(h + 1) * DV] = (on * _silu(g_ref[:, h * DV:(h + 1) * DV])).astype(out_ref.dtype)
    yield


def _gla_packed_kernel(qk_ref, v_ref, g_ref, la_ref, hw_ref, code_ref, s_in, *rest, seg):
    out_ref, s_ref = rest[-2:]
    b2_all = _gla_prologue(la_ref, seg)
    for h in range(N_HEADS):
        for _ in _gla_head(h, b2_all[:, h * DK:(h + 1) * DK], (qk_ref, v_ref, g_ref), hw_ref, code_ref, s_in, s_ref,
                           out_ref, rows=PACKED_ROWS, seg=seg):
            pass


def _gla_packed_call(qk, v, g, la, w, j, batch, seq, s_all, s_stack):
    rows, seg = PACKED_ROWS, seq
    nseg = rows // seg
    tile = lambda width: pl.BlockSpec((rows, width), lambda b: (b, 0))
    s_spec = pl.BlockSpec((None, nseg, N_HEADS, DK, DV), lambda b: (j, b, 0, 0, 0))
    in_specs = [tile(2 * QK_W), tile(V_W), tile(V_W), tile(QK_W), _resident((1, V_W), (j,)),
                _resident((rows, rows)), s_spec]
    args = [qk, v, g, la, w["gla"]["hnorm_w"], _gla_codes(rows, seg), s_all]
    aliases = {}
    if s_stack is not None:
        in_specs.append(pl.BlockSpec(memory_space=pl.ANY))
        args.append(s_stack)
        aliases = {len(args) - 1: 1}
    return pl.pallas_call(
        functools.partial(_gla_packed_kernel, seg=seg),
        grid=(batch * seq // rows,),
        in_specs=in_specs,
        out_specs=[tile(V_W), s_spec],
        out_shape=[jax.ShapeDtypeStruct((batch * seq, V_W), BF16), jax.ShapeDtypeStruct(s_all.shape, F32)],
        input_output_aliases=aliases,
        compiler_params=pltpu.CompilerParams(dimension_semantics=("parallel",), vmem_limit_bytes=VMEM_LIMIT_BYTES),
        name="gla_packed",
    )(*args)


def _scanpost_kernel(*refs, is_gla, rows, tiles_per_seq, ntiles, final):
    n_scan_w = 2 if is_gla else 4
    n_post_w = 6 if final else 5
    n_state = 1 if is_gla else 4
    tile_refs = refs[0:4]
    scan_w = refs[4:4 + n_scan_w]
    x_ref = refs[4 + n_scan_w]
    wout_ref, nw_ref, wg_ref, wu_ref, wd_ref = refs[5 + n_scan_w:10 + n_scan_w]
    o_ref = refs[5 + n_scan_w + n_post_w]
    state_out = refs[6 + n_scan_w + n_post_w:6 + n_scan_w + n_post_w + n_state]
    scratch = refs[6 + n_scan_w + n_post_w + n_state:]
    gated_ref, h_ref = scratch[0], scratch[1]

    step = pl.program_id(0)
    valid = step < ntiles
    first = jnp.logical_and(valid, lax.rem(step, tiles_per_seq) == 0)

    @pl.when(step == 0)
    def _():
        gated_ref[...] = jnp.zeros(gated_ref.shape, gated_ref.dtype)

    if is_gla:
        _gla_init(first, state_out[0])
    else:
        _mlstm_init(first, state_out, scratch[2])

    def scan_parts():
        if is_gla:
            b2_all = {r0: _gla_prologue(tile_refs[3].at[r0:r0 + GLA_ROWS, :], GLA_ROWS)
                      for r0 in range(0, rows, GLA_ROWS)}
            yield
            for h in range(N_HEADS):
                for r0 in range(0, rows, GLA_ROWS):
                    part_refs = tuple(r.at[r0:r0 + GLA_ROWS, :] for r in tile_refs[:3])
                    yield from _gla_head(h, b2_all[r0][:, h * DK:(h + 1) * DK], part_refs, scan_w[0],
                                         scan_w[1], None, state_out[0], gated_ref.at[r0:r0 + GLA_ROWS, :],
                                         rows=GLA_ROWS, seg=GLA_ROWS, valid=valid)
        else:
            ctx = _mlstm_prologue(tile_refs, scan_w, None, state_out, scratch[2], rows=rows, seg=rows, valid=valid)
            yield
            for h in range(N_HEADS):
                yield from _mlstm_head(h, ctx, tile_refs, scan_w, None, state_out, gated_ref, rows=rows, seg=rows,
                                       valid=valid)

    n_scan = 1 + N_HEADS * (5 * (rows // GLA_ROWS) if is_gla else 4)
    post = {}

    def out_proj():
        post["x2"] = x_ref[...] + jnp.dot(gated_ref[...], wout_ref[...], preferred_element_type=F32)
        post["xn"] = _rms(post["x2"], nw_ref[2:3, :]).astype(BF16)

    def ffn_cols(c):
        g = jnp.dot(post["xn"], wg_ref[:, c:c + FFN_COLS], preferred_element_type=F32)
        u = jnp.dot(post["xn"], wu_ref[:, c:c + FFN_COLS], preferred_element_type=F32)
        h_ref[:, c:c + FFN_COLS] = (_silu(g) * u).astype(BF16)

    def down_proj(lo, hi):
        part = jnp.dot(h_ref[:, lo:hi], wd_ref[lo:hi, :], preferred_element_type=F32)
        post["y"] = part if "y" not in post else post["y"] + part

    post_items = [out_proj]
    n_cols = D_FF // FFN_COLS
    for p in range(N_HEADS):
        lo, hi = FFN_COLS * (p * n_cols // N_HEADS), FFN_COLS * ((p + 1) * n_cols // N_HEADS)
        post_items += [functools.partial(ffn_cols, c) for c in range(lo, hi, FFN_COLS)]
        post_items.append(functools.partial(down_proj, lo, hi))

    lead = 2 if is_gla else 4
    post_items[0]()
    done = 1
    for i, _ in enumerate(scan_parts()):
        upto = lead + i * (len(post_items) - lead) // (n_scan - 1)
        for item in post_items[done:upto]:
            item()
        done = upto
    assert done == len(post_items)

    x3 = post["x2"] + 0.5 * post["y"]
    o_ref[...] = _rms(x3, refs[10 + n_scan_w][...]) if final else x3


def _scanpost_call(x1, qk, v, og, sm, w, l, batch, seq):
    is_gla = l % 2 == 1
    j = l // 2
    final = l == DEPTH - 1
    rows = TOKEN_TILE
    tiles_per_seq = seq // rows
    ntiles = batch * tiles_per_seq
    cur = lambda width: pl.BlockSpec((rows, width), lambda s: (jnp.minimum(s, ntiles - 1), 0))
    prev = pl.BlockSpec((rows, D_MODEL), lambda s: (jnp.maximum(s - 1, 0), 0))
    per_seq = lambda shape: pl.BlockSpec(
        (1,) + shape, lambda s: (jnp.minimum(s, ntiles - 1) // tiles_per_seq,) + (0,) * len(shape))
    if is_gla:
        scan_specs = [_resident((1, V_W), (j,)), _resident((GLA_ROWS, GLA_ROWS))]
        scan_args = [w["gla"]["hnorm_w"], _gla_codes(GLA_ROWS, GLA_ROWS)]
        state_specs = [per_seq((N_HEADS, DK, DV))]
        state_shapes = [jax.ShapeDtypeStruct((batch, N_HEADS, DK, DV), F32)]
        scratch = []
        sm_w = QK_W
    else:
        scan_specs, scan_args = _mlstm_weight_specs(w, j)
        state_specs = [per_seq((CONV_W - 1, 2 * QK_W)), per_seq((N_HEADS, DK, DV)), per_seq((1, QK_W)),
                       per_seq((1, N_HEADS))]
        state_shapes = [jax.ShapeDtypeStruct((batch, CONV_W - 1, 2 * QK_W), F32),
                        jax.ShapeDtypeStruct((batch, N_HEADS, DK, DV), F32),
                        jax.ShapeDtypeStruct((batch, 1, QK_W), F32),
                        jax.ShapeDtypeStruct((batch, 1, N_HEADS), F32)]
        scratch = [pltpu.VMEM((1, rows + SUBLANES, 2 * QK_W), F32)]
        sm_w = LANES
    post_specs, post_args = _post_specs(w, l)
    return pl.pallas_call(
        functools.partial(_scanpost_kernel, is_gla=is_gla, rows=rows, tiles_per_seq=tiles_per_seq, ntiles=ntiles,
                          final=final),
        grid=(ntiles + 1,),
        in_specs=[cur(2 * QK_W), cur(V_W), cur(V_W), cur(sm_w)] + scan_specs + [prev] + post_specs,
        out_specs=[prev] + state_specs,
        out_shape=[jax.ShapeDtypeStruct((batch * seq, D_MODEL), F32)] + state_shapes,
        scratch_shapes=[pltpu.VMEM((rows, V_W), BF16), pltpu.VMEM((rows, D_FF), BF16)] + scratch,
        compiler_params=pltpu.CompilerParams(dimension_semantics=("arbitrary",), vmem_limit_bytes=VMEM_LIMIT_BYTES),
        name=("gla" if is_gla else "mlstm") + "_scanpost",
    )(qk, v, og, sm, *scan_args, x1, *post_args)


def _prepare_weights(norm_w, final_norm_w, ffn_w_gate, ffn_w_up, ffn_w_down,
                     mlstm_w_in, mlstm_conv_w, mlstm_conv_b, mlstm_b_i, mlstm_b_f, mlstm_hnorm_w, mlstm_w_out,
                     gla_w_in, gla_w_a2, gla_b_a, gla_hnorm_w, gla_w_out):
    def small(w_in, n):
        return jnp.pad(w_in[:, :, PROJ_W:], ((0, 0), (0, 0), (0, LANES - n))).astype(BF16)

    gate_b = jnp.pad(jnp.concatenate([mlstm_b_i, mlstm_b_f], axis=1), ((0, 0), (0, LANES - 2 * N_HEADS)))
    return {
        "norm": norm_w, "final": final_norm_w[None, :],
        "gate": _to_bf16(ffn_w_gate.reshape(2 * DEPTH, D_MODEL, D_FF)),
        "up": _to_bf16(ffn_w_up.reshape(2 * DEPTH, D_MODEL, D_FF)),
        "down": _to_bf16(ffn_w_down.reshape(2 * DEPTH, D_FF, D_MODEL)),
        "mlstm": {
            "w_in": _to_bf16(mlstm_w_in), "w_small": small(mlstm_w_in, 2 * N_HEADS),
            "w_out": _to_bf16(mlstm_w_out), "conv_w": mlstm_conv_w, "conv_b": mlstm_conv_b[:, None, :],
            "gate_b": gate_b[:, None, :], "hnorm_w": mlstm_hnorm_w[:, None, :],
        },
        "gla": {
            "w_in": _to_bf16(gla_w_in), "w_small": small(gla_w_in, GATE_RANK), "w_out": _to_bf16(gla_w_out),
            "w_a2": jnp.pad(gla_w_a2, ((0, 0), (0, LANES - GATE_RANK), (0, 0))).astype(BF16),
            "b_a": gla_b_a[:, None, :], "hnorm_w": gla_hnorm_w[:, None, :],
        },
    }


def _prompt_trunk(x, w):
    batch, seq, _ = x.shape
    x = x.reshape(batch * seq, D_MODEL)
    new_c, new_n, new_m, new_conv, new_s = [], [], [], [], []
    for l in range(DEPTH):
        x1, qk, v, og, sm = _pre_call(x, w, l)
        if l % 2 == 1:
            x, s_new = _scanpost_call(x1, qk, v, og, sm, w, l, batch, seq)
            new_s.append(s_new)
        else:
            x, conv_new, c_new, n_new, m_new = _scanpost_call(x1, qk, v, og, sm, w, l, batch, seq)
            new_c.append(c_new); new_conv.append(conv_new)
            new_n.append(n_new.reshape(batch, N_HEADS, DK)); new_m.append(m_new.reshape(batch, N_HEADS))
    return (x.reshape(batch, seq, D_MODEL), jnp.stack(new_c), jnp.stack(new_n), jnp.stack(new_m),
            jnp.stack(new_conv), jnp.stack(new_s))


def _sample_trunk(x, w, c_all, n_all, m_all, conv_all, s_all):
    batch, seq, _ = x.shape
    x = x.reshape(batch * seq, D_MODEL)
    new_n, new_m, new_conv = [], [], []
    c_stack = s_stack = None
    for l in range(DEPTH):
        j = l // 2
        x1, qk, v, og, sm = _pre_call(x, w, l)
        if l % 2 == 1:
            gated, s_stack = _gla_packed_call(qk, v, og, sm, w, j, batch, seq, s_all, s_stack)
        else:
            gated, conv_new, c_stack, n_new, m_new = _mlstm_packed_call(
                qk, v, og, sm, w, j, batch, seq, conv_all, c_all, n_all[j], m_all[j], c_stack)
            new_n.append(n_new); new_m.append(m_new); new_conv.append(conv_new)
        x = _post_call(x1, gated, w, l)
    return (x.reshape(batch, seq, D_MODEL), c_stack, jnp.stack(new_n), jnp.stack(new_m), jnp.stack(new_conv),
            s_stack)


def kernel(x_prompt, x_sample, state_mlstm_C, state_mlstm_n, state_mlstm_m, state_mlstm_conv, state_gla_S, norm_w, final_norm_w, ffn_w_gate, ffn_w_up, ffn_w_down, mlstm_w_in, mlstm_conv_w, mlstm_conv_b, mlstm_b_i, mlstm_b_f, mlstm_hnorm_w, mlstm_w_out, gla_w_in, gla_w_a2, gla_b_a, gla_hnorm_w, gla_w_out):
    w = _prepare_weights(norm_w, final_norm_w, ffn_w_gate, ffn_w_up, ffn_w_down,
                         mlstm_w_in, mlstm_conv_w, mlstm_conv_b, mlstm_b_i, mlstm_b_f, mlstm_hnorm_w, mlstm_w_out,
                         gla_w_in, gla_w_a2, gla_b_a, gla_hnorm_w, gla_w_out)
    y_p, c_p, n_p, m_p, conv_p, s_p = _prompt_trunk(x_prompt, w)
    y_s, c_s, n_s, m_s, conv_s, s_s = _sample_trunk(
        x_sample, w, state_mlstm_C, state_mlstm_n, state_mlstm_m, state_mlstm_conv, state_gla_S)
    return (y_p, y_s, c_p, n_p, m_p, conv_p, s_p, c_s, n_s, m_s, conv_s, s_s)
```

```python
import functools
import math

import jax
import jax.numpy as jnp
import numpy as np
from jax import lax
from jax.experimental import pallas as pl
from jax.experimental.pallas import tpu as pltpu

D_MODEL = 1024
DEPTH = 4
N_HEADS = 4
DK = 128
DV = 256
QK_W = N_HEADS * DK
V_W = N_HEADS * DV
CONV_W = 4
GATE_RANK = 16
GLA_NORMALIZER = 16.0
D_FF = 2816
EPS = 1e-6
PROJ_W = 2 * QK_W + 2 * V_W
LOG2E = math.log2(math.e)

LANES = 128
SUBLANES = 8
VMEM_LIMIT_BYTES = 58 * 1024 * 1024

TOKEN_TILE = 256
PRE_TILE = 512
FFN_COLS = 256
GLA_ROWS = 128
PACKED_ROWS = 128

BF16 = jnp.bfloat16
F32 = jnp.float32


def _resident(shape, index=None):
    index = tuple(index or ())
    block = (None,) * len(index) + tuple(shape)
    return pl.BlockSpec(block, lambda *_: index + (0,) * len(shape), pipeline_mode=pl.Buffered(1))


def _rms(x, w):
    return x * lax.rsqrt(jnp.mean(x * x, axis=-1, keepdims=True) + EPS) * w


def _silu(x):
    return x * jax.nn.sigmoid(x)


def _log_sigmoid(x):
    return jnp.minimum(x, 0.0) - jnp.log1p(jnp.exp(-jnp.abs(x)))


def _dot(a, b):
    return jnp.dot(a.astype(BF16), b.astype(BF16), preferred_element_type=F32)


def _dot_nt(a, b):
    return lax.dot_general(a.astype(BF16), b.astype(BF16), (((1,), (1,)), ((), ())), preferred_element_type=F32)


def _dot_tn(a, b):
    return lax.dot_general(a.astype(BF16), b.astype(BF16), (((0,), (0,)), ((), ())), preferred_element_type=F32)


def _seg_cumsum(x, seg):
    n = x.shape[0]
    row = lax.broadcasted_iota(jnp.int32, (n, n), 0)
    col = lax.broadcasted_iota(jnp.int32, (n, n), 1)
    shift = seg.bit_length() - 1
    tri = jnp.where((col <= row) & ((row >> shift) == (col >> shift)), 1.0, 0.0).astype(BF16)
    hi = x.astype(BF16)
    r1 = x - hi.astype(F32)
    mid = r1.astype(BF16)
    lo = (r1 - mid.astype(F32)).astype(BF16)
    return jnp.dot(jnp.concatenate([tri, tri, tri], axis=1), jnp.concatenate([hi, mid, lo], axis=0),
                   preferred_element_type=F32)


def _group_row(x, group, idx):
    n, c = x.shape
    if n == group:
        return x[idx:idx + 1, :]
    return jnp.concatenate(
        [jnp.broadcast_to(x[g * group + idx:g * group + idx + 1, :], (group, c)) for g in range(n // group)], axis=0)


def _group_max(x, group):
    n, c = x.shape
    if n == group:
        return jnp.max(x, axis=0, keepdims=True)
    return jnp.concatenate(
        [jnp.broadcast_to(jnp.max(x[g * group:(g + 1) * group, :], axis=0, keepdims=True), (group, c))
         for g in range(n // group)], axis=0)


def _col_to_row(col):
    n = col.shape[0]
    eye = lax.broadcasted_iota(jnp.int32, (n, n), 0) == lax.broadcasted_iota(jnp.int32, (n, n), 1)
    return jnp.sum(jnp.where(eye, col, 0.0), axis=0, keepdims=True)


def _row_to_col(row):
    n = row.shape[1]
    eye = lax.broadcasted_iota(jnp.int32, (n, n), 0) == lax.broadcasted_iota(jnp.int32, (n, n), 1)
    return jnp.sum(jnp.where(eye, row, 0.0), axis=1, keepdims=True)


def _keep(valid, new, old_ref_value):
    return new if valid is None else jnp.where(valid, new, old_ref_value)


def _cast_kernel(w_ref, o_ref):
    o_ref[...] = w_ref[...].astype(o_ref.dtype)


def _to_bf16(w):
    n, r, c = w.shape
    spec = pl.BlockSpec((1, r, c), lambda i: (i, 0, 0))
    return pl.pallas_call(
        _cast_kernel, grid=(n,), in_specs=[spec], out_specs=spec,
        out_shape=jax.ShapeDtypeStruct(w.shape, BF16),
        compiler_params=pltpu.CompilerParams(dimension_semantics=("parallel",), vmem_limit_bytes=VMEM_LIMIT_BYTES),
        name="cast_bf16",
    )(w)


def _ffn(x, nw, wg_ref, wu_ref, wd_ref, h_ref):
    xn = _rms(x, nw).astype(BF16)
    for c in range(0, D_FF, FFN_COLS):
        g = jnp.dot(xn, wg_ref[:, c:c + FFN_COLS], preferred_element_type=F32)
        u = jnp.dot(xn, wu_ref[:, c:c + FFN_COLS], preferred_element_type=F32)
        h_ref[:, c:c + FFN_COLS] = (_silu(g) * u).astype(BF16)
    return x + 0.5 * jnp.dot(h_ref[...], wd_ref[...], preferred_element_type=F32)


def _pre_kernel(*refs, is_gla):
    if is_gla:
        (x_ref, nw_ref, wg_ref, wu_ref, wd_ref, win_ref, wsm_ref, wa2_ref, ba_ref,
         x1_ref, qk_ref, v_ref, og_ref, sm_ref, h_ref) = refs
    else:
        (x_ref, nw_ref, wg_ref, wu_ref, wd_ref, win_ref, wsm_ref,
         x1_ref, qk_ref, v_ref, og_ref, sm_ref, h_ref) = refs
    x1 = _ffn(x_ref[...], nw_ref[0:1, :], wg_ref, wu_ref, wd_ref, h_ref)
    x1_ref[...] = x1
    xn = _rms(x1, nw_ref[1:2, :]).astype(BF16)
    for i, p_ref in enumerate((qk_ref, v_ref, og_ref)):
        p_ref[...] = jnp.dot(xn, win_ref[:, i * D_MODEL:(i + 1) * D_MODEL],
                             preferred_element_type=F32).astype(p_ref.dtype)
    small = jnp.dot(xn, wsm_ref[...], preferred_element_type=F32)
    if is_gla:
        a = jnp.dot(small.astype(BF16), wa2_ref[...], preferred_element_type=F32) + ba_ref[...]
        sm_ref[...] = _log_sigmoid(a) * (1.0 / GLA_NORMALIZER)
    else:
        sm_ref[...] = small


def _pre_call(x, w, l):
    m = x.shape[0]
    is_gla = l % 2 == 1
    j = l // 2
    tile = lambda width: pl.BlockSpec((PRE_TILE, width), lambda i: (i, 0))
    sm_w = QK_W if is_gla else LANES
    mix = w["gla"] if is_gla else w["mlstm"]
    in_specs = [tile(D_MODEL), _resident((3, D_MODEL), (l,)), _resident((D_MODEL, D_FF), (2 * l,)),
                _resident((D_MODEL, D_FF), (2 * l,)), _resident((D_FF, D_MODEL), (2 * l,)),
                _resident((D_MODEL, PROJ_W), (j,)), _resident((D_MODEL, LANES), (j,))]
    args = [x, w["norm"], w["gate"], w["up"], w["down"], mix["w_in"], mix["w_small"]]
    if is_gla:
        in_specs += [_resident((LANES, QK_W), (j,)), _resident((1, QK_W), (j,))]
        args += [mix["w_a2"], mix["b_a"]]
    return pl.pallas_call(
        functools.partial(_pre_kernel, is_gla=is_gla),
        grid=(m // PRE_TILE,),
        in_specs=in_specs,
        out_specs=[tile(D_MODEL), tile(D_MODEL), tile(D_MODEL), tile(D_MODEL), tile(sm_w)],
        out_shape=[jax.ShapeDtypeStruct((m, D_MODEL), F32), jax.ShapeDtypeStruct((m, D_MODEL), F32),
                   jax.ShapeDtypeStruct((m, D_MODEL), BF16), jax.ShapeDtypeStruct((m, D_MODEL), F32),
                   jax.ShapeDtypeStruct((m, sm_w), F32)],
        scratch_shapes=[pltpu.VMEM((PRE_TILE, D_FF), BF16)],
        compiler_params=pltpu.CompilerParams(dimension_semantics=("parallel",), vmem_limit_bytes=VMEM_LIMIT_BYTES),
        name="pre_gla" if is_gla else "pre_mlstm",
    )(*args)


def _post(x1, gated, post_refs, h_ref, final):
    wout_ref, nw_ref, wg_ref, wu_ref, wd_ref = post_refs[:5]
    x2 = x1 + jnp.dot(gated, wout_ref[...], preferred_element_type=F32)
    x3 = _ffn(x2, nw_ref[2:3, :], wg_ref, wu_ref, wd_ref, h_ref)
    return _rms(x3, post_refs[5][...]) if final else x3


def _post_specs(w, l):
    is_gla = l % 2 == 1
    j = l // 2
    mix = w["gla"] if is_gla else w["mlstm"]
    specs = [_resident((V_W, D_MODEL), (j,)), _resident((3, D_MODEL), (l,)),
             _resident((D_MODEL, D_FF), (2 * l + 1,)), _resident((D_MODEL, D_FF), (2 * l + 1,)),
             _resident((D_FF, D_MODEL), (2 * l + 1,))]
    args = [mix["w_out"], w["norm"], w["gate"], w["up"], w["down"]]
    if l == DEPTH - 1:
        specs.append(_resident((1, D_MODEL)))
        args.append(w["final"])
    return specs, args


def _post_kernel(*refs, final):
    x_ref, gated_ref = refs[:2]
    o_ref, h_ref = refs[-2:]
    o_ref[...] = _post(x_ref[...], gated_ref[...], refs[2:-2], h_ref, final)


def _post_call(x1, gated, w, l):
    m = x1.shape[0]
    final = l == DEPTH - 1
    tile = pl.BlockSpec((TOKEN_TILE, D_MODEL), lambda i: (i, 0))
    specs, args = _post_specs(w, l)
    return pl.pallas_call(
        functools.partial(_post_kernel, final=final),
        grid=(m // TOKEN_TILE,),
        in_specs=[tile, tile] + specs,
        out_specs=tile,
        out_shape=jax.ShapeDtypeStruct((m, D_MODEL), F32),
        scratch_shapes=[pltpu.VMEM((TOKEN_TILE, D_FF), BF16)],
        compiler_params=pltpu.CompilerParams(dimension_semantics=("parallel",), vmem_limit_bytes=VMEM_LIMIT_BYTES),
        name="post_final" if final else "post",
    )(x1, gated, *args)


def _mlstm_init(first, state_out, ext_ref):
    conv_out, c_ref, n_ref, m_ref = state_out

    @pl.when(first)
    def _():
        ext_ref[0, 0:SUBLANES, :] = jnp.zeros((SUBLANES, 2 * QK_W), F32)
        conv_out[...] = jnp.zeros(conv_out.shape, F32)
        c_ref[...] = jnp.zeros(c_ref.shape, F32)
        n_ref[...] = jnp.zeros(n_ref.shape, F32)
        m_ref[...] = jnp.zeros(m_ref.shape, F32)


def _mlstm_prologue(tile_refs, w_refs, state_in, state_out, ext_ref, *, rows, seg, valid=None):
    qk_ref, gt_ref = tile_refs[0], tile_refs[3]
    cw_ref, cb_ref, gb_ref = w_refs[:3]
    conv_out = state_out[0]
    carry = state_in is None
    nseg = rows // seg
    hist = SUBLANES

    pieces = []
    for i in range(nseg):
        if not carry:
            ext_ref[i, hist - (CONV_W - 1):hist, :] = state_in[0][i]
        ext_ref[i, hist:hist + seg, :] = qk_ref[i * seg:(i + 1) * seg, :]
        acc = cb_ref[...] + cw_ref[CONV_W - 1:CONV_W, :] * ext_ref[i, hist:hist + seg, :]
        for j in range(1, CONV_W):
            acc = acc + cw_ref[CONV_W - 1 - j:CONV_W - j, :] * ext_ref[i, hist - j:hist - j + seg, :]
        conv_out[i] = _keep(valid, ext_ref[i, hist + seg - (CONV_W - 1):hist + seg, :], conv_out[i])
        pieces.append(acc)
    if carry:
        ext_ref[0, 0:hist, :] = ext_ref[0, seg:seg + hist, :]
    qk = _silu(pieces[0] if nseg == 1 else jnp.concatenate(pieces, axis=0))

    gt = gt_ref[...] + gb_ref[...]
    k = qk[:, QK_W:] * (DK ** -0.5)
    return qk[:, :QK_W], k, qk[:, :QK_W].astype(BF16), k.astype(BF16), gt, _seg_cumsum(_log_sigmoid(gt), seg)


def _mlstm_head(h, ctx, tile_refs, w_refs, state_in, state_out, out_ref, *, rows, seg, valid=None):
    q_all, k_all, qb_all, kb_all, gt, bcum = ctx
    v_ref, o_ref = tile_refs[1], tile_refs[2]
    hw_ref = w_refs[3]
    _, c_ref, n_ref, m_ref = state_out
    carry = state_in is None
    nseg = rows // seg
    row = lax.broadcasted_iota(jnp.int32, (rows, rows), 0)
    col = lax.broadcasted_iota(jnp.int32, (rows, rows), 1)
    shift = seg.bit_length() - 1
    causal = (col <= row) & ((row >> shift) == (col >> shift))

    q = q_all[:, h * DK:(h + 1) * DK]
    k = k_all[:, h * DK:(h + 1) * DK]
    qb = qb_all[:, h * DK:(h + 1) * DK]
    v = v_ref[:, h * DV:(h + 1) * DV]
    ic = gt[:, h:h + 1]
    b = bcum[:, N_HEADS + h:N_HEADS + h + 1]
    if carry:
        m_prev = m_ref[0, :, h:h + 1]
        n_prev = n_ref[0, :, h * DK:(h + 1) * DK]
    else:
        m_prev = state_in[3][:, h:h + 1]
        n_prev = state_in[2][:, h * DK:(h + 1) * DK]
    c_prev = [c_ref[0, h] if carry else state_in[1][i, h] for i in range(nseg)]

    qk_t = _dot_nt(qb, kb_all[:, h * DK:(h + 1) * DK])
    inter = [_dot(qb[i * seg:(i + 1) * seg, :], c_prev[i]) for i in range(nseg)]
    inter = inter[0] if nseg == 1 else jnp.concatenate(inter, axis=0)
    a_row = _col_to_row(ic - b)
    dm = jnp.where(causal, b + a_row, -jnp.inf)
    mt = jnp.maximum(b + m_prev, jnp.max(dm, axis=1, keepdims=True))
    w_inter = jnp.exp(b + m_prev - mt)
    p = jnp.exp(dm - mt)
    b_last = _group_row(b, seg, seg - 1)
    g = b_last - b + ic
    m_new = jnp.maximum(b_last + m_prev, _group_max(g, seg))
    sc_prev = jnp.exp(b_last + m_prev - m_new)
    kw = k * jnp.exp(g - m_new)
    kw_b = kw.astype(BF16)
    yield

    s = qk_t * p
    s_b = s.astype(BF16)
    yield

    intra = _dot(s_b, v)
    kv = [_dot_tn(kw_b[i * seg:(i + 1) * seg, :], v[i * seg:(i + 1) * seg, :]) for i in range(nseg)]
    yield

    num = intra + w_inter * inter
    den = jnp.sum(s, axis=1, keepdims=True) + w_inter * jnp.sum(q * n_prev, axis=1, keepdims=True)
    hh = num / jnp.maximum(jnp.abs(den), jnp.exp(-mt))
    for i in range(nseg):
        lo, hi = i * seg, (i + 1) * seg
        sc_i = sc_prev[lo:lo + 1, :]
        c_ref[i, h] = _keep(valid, sc_i * c_prev[i] + kv[i], c_prev[i])
        n_old = n_prev[lo:lo + 1, :]
        n_ref[i, :, h * DK:(h + 1) * DK] = _keep(
            valid, sc_i * n_old + jnp.sum(kw[lo:hi, :], axis=0, keepdims=True), n_old)
        m_ref[i, :, h:h + 1] = _keep(valid, m_new[lo:lo + 1, :], m_prev[lo:lo + 1, :])
    hn = _rms(hh, hw_ref[:, h * DV:(h + 1) * DV])
    out_ref[:, h * DV:(h + 1) * DV] = (hn * jax.nn.sigmoid(o_ref[:, h * DV:(h + 1) * DV])).astype(out_ref.dtype)
    yield


def _mlstm_weight_specs(w, j):
    mw = w["mlstm"]
    specs = [_resident((CONV_W, 2 * QK_W), (j,)), _resident((1, 2 * QK_W), (j,)), _resident((1, LANES), (j,)),
             _resident((1, V_W), (j,))]
    return specs, [mw["conv_w"], mw["conv_b"], mw["gate_b"], mw["hnorm_w"]]


def _mlstm_packed_kernel(*refs, seg, aliased):
    outs = list(refs[13:] if aliased else refs[12:])
    if not aliased:
        stack_ref = outs[2]
        stack_ref[1:] = jnp.zeros((stack_ref.shape[0] - 1,) + stack_ref.shape[1:], F32)
        outs[2] = stack_ref.at[0]
    ctx = _mlstm_prologue(refs[0:4], refs[4:8], refs[8:12], outs[1:5], outs[5], rows=PACKED_ROWS, seg=seg)
    for h in range(N_HEADS):
        for _ in _mlstm_head(h, ctx, refs[0:4], refs[4:8], refs[8:12], outs[1:5], outs[0], rows=PACKED_ROWS, seg=seg):
            pass


def _mlstm_packed_call(qk, v, o, gates, w, j, batch, seq, conv_all, c_all, n0, m0, c_stack):
    rows, seg = PACKED_ROWS, seq
    nseg = rows // seg
    tile = lambda width: pl.BlockSpec((rows, width), lambda b: (b, 0))
    per_seq = lambda shape: pl.BlockSpec((nseg,) + shape, lambda b: (b,) + (0,) * len(shape))
    layer_seq = lambda shape: pl.BlockSpec((None, nseg) + shape, lambda b: (j, b) + (0,) * len(shape))
    w_specs, w_args = _mlstm_weight_specs(w, j)
    in_specs = ([tile(2 * QK_W), tile(V_W), tile(V_W), tile(LANES)] + w_specs
                + [layer_seq((CONV_W - 1, 2 * QK_W)), layer_seq((N_HEADS, DK, DV)), tile(QK_W), tile(N_HEADS)])
    args = [qk, v, o, gates] + w_args + [conv_all, c_all, jnp.repeat(n0.reshape(batch, QK_W), seq, axis=0),
                                         jnp.repeat(m0, seq, axis=0)]
    aliases = {}
    if c_stack is None:
        c_spec = pl.BlockSpec((c_all.shape[0], nseg, N_HEADS, DK, DV), lambda b: (0, b, 0, 0, 0))
    else:
        c_spec = layer_seq((N_HEADS, DK, DV))
        in_specs.append(pl.BlockSpec(memory_space=pl.ANY))
        args.append(c_stack)
        aliases = {len(args) - 1: 2}
    out, conv_new, c_stack, n_new, m_new = pl.pallas_call(
        functools.partial(_mlstm_packed_kernel, seg=seg, aliased=bool(aliases)),
        grid=(batch * seq // rows,),
        in_specs=in_specs,
        out_specs=[tile(V_W), per_seq((CONV_W - 1, 2 * QK_W)), c_spec, per_seq((1, QK_W)), per_seq((1, N_HEADS))],
        out_shape=[jax.ShapeDtypeStruct((batch * seq, V_W), BF16),
                   jax.ShapeDtypeStruct((batch, CONV_W - 1, 2 * QK_W), F32),
                   jax.ShapeDtypeStruct(c_all.shape, F32),
                   jax.ShapeDtypeStruct((batch, 1, QK_W), F32),
                   jax.ShapeDtypeStruct((batch, 1, N_HEADS), F32)],
        scratch_shapes=[pltpu.VMEM((nseg, seg + SUBLANES, 2 * QK_W), F32)],
        input_output_aliases=aliases,
        compiler_params=pltpu.CompilerParams(dimension_semantics=("parallel",), vmem_limit_bytes=VMEM_LIMIT_BYTES),
        name="mlstm_packed",
    )(*args)
    return out, conv_new, c_stack, n_new.reshape(batch, N_HEADS, DK), m_new.reshape(batch, N_HEADS)


def _gla_init(first, s_ref):
    @pl.when(first)
    def _():
        s_ref[...] = jnp.zeros(s_ref.shape, F32)


def _gla_halves(seg):
    halves, half = [], seg // 2
    while half >= SUBLANES:
        halves.append(half)
        half //= 2
    return halves


def _gla_codes(rows, seg):
    sub = min(SUBLANES, seg)
    r = np.arange(rows)[:, None]
    c = np.arange(rows)[None, :]
    code = np.full((rows, rows), -1, np.int32)
    halves = _gla_halves(seg)
    for level, half in enumerate(halves):
        grp = 2 * half
        code[(r // grp == c // grp) & (r % grp >= half) & (c % grp < half)] = level
    inside = (r // sub == c // sub) & (c <= r)
    return np.where(inside, len(halves) + (c % sub), code).astype(np.int32)


def _gla_prologue(la_ref, seg):
    return _seg_cumsum(la_ref[...], seg) * LOG2E


def _gla_head(h, b2, tile_refs, hw_ref, code_ref, s_in, s_ref, out_ref, *, rows, seg, valid=None):
    qk_ref, v_ref, g_ref = tile_refs[:3]
    carry = s_in is None
    nseg = rows // seg
    sub = min(SUBLANES, seg)
    halves = _gla_halves(seg)
    code = code_ref[...]
    rowk = lax.broadcasted_iota(jnp.int32, (rows, DK), 0)

    q = qk_ref[:, h * DK:(h + 1) * DK] * (DK ** -0.5)
    k = qk_ref[:, QK_W + h * DK:QK_W + (h + 1) * DK]
    v = v_ref[:, h * DV:(h + 1) * DV]
    s_prev = [s_ref[0, h] if carry else s_in[i, h] for i in range(nseg)]

    qd = (q * jnp.exp2(b2)).astype(BF16)
    s_prev_b = [s.astype(BF16) for s in s_prev]
    xs = []
    for half in halves:
        second = (rowk & (2 * half - 1)) >= half
        ref = _group_row(b2, 2 * half, half - 1)
        xs.append((jnp.where(second, q, k) * jnp.exp2(-jnp.abs(b2 - ref))).astype(BF16))
    yield

    if nseg > 1:
        inter = jnp.concatenate([_dot(qd[i * seg:(i + 1) * seg, :], s_prev_b[i]) for i in range(nseg)], axis=0)
    cross = [_dot_nt(x, x) for x in xs]
    wgts = []
    for s in range(sub):
        dec = jnp.exp2(b2 - _group_row(b2, sub, s))
        wgts.append(jnp.sum(q * _group_row(k, sub, s) * dec, axis=1, keepdims=True))
    b_last = _group_row(b2, seg, seg - 1)
    k_dec = (k * jnp.exp2(b_last - b2)).astype(BF16)
    yield

    att = jnp.zeros((rows, rows), F32)
    for level, prod in enumerate(cross):
        att = jnp.where(code == level, prod, att)
    for s, wgt in enumerate(wgts):
        att = jnp.where(code == len(halves) + s, wgt, att)
    att = att.astype(BF16)
    yield

    if nseg > 1:
        o = inter + _dot(att, v)
    else:
        o = jnp.dot(jnp.concatenate([qd, att], axis=1), jnp.concatenate([s_prev_b[0], v], axis=0),
                    preferred_element_type=F32)
    kv = [_dot_tn(k_dec[i * seg:(i + 1) * seg, :], v[i * seg:(i + 1) * seg, :]) for i in range(nseg)]
    yield

    for i in range(nseg):
        decay = _row_to_col(jnp.exp2(b_last[i * seg:i * seg + 1, :]))
        s_ref[i, h] = _keep(valid, decay * s_prev[i] + kv[i], s_prev[i])
    on = _rms(o, hw_ref[:, h * DV:(h + 1) * DV])
    out_ref[:, h * DV:(h + 1) * DV] = (on * _silu(g_ref[:, h * DV:(h + 1) * DV])).astype(out_ref.dtype)
    yield


def _gla_packed_kernel(qk_ref, v_ref, g_ref, la_ref, hw_ref, code_ref, s_in, *rest, seg, aliased):
    out_ref, s_ref = rest[-2:]
    if not aliased:
        s_ref[1:] = jnp.zeros((s_ref.shape[0] - 1,) + s_ref.shape[1:], F32)
        s_ref = s_ref.at[0]
    b2_all = _gla_prologue(la_ref, seg)
    for h in range(N_HEADS):
        for _ in _gla_head(h, b2_all[:, h * DK:(h + 1) * DK], (qk_ref, v_ref, g_ref), hw_ref, code_ref, s_in, s_ref,
                           out_ref, rows=PACKED_ROWS, seg=seg):
            pass


def _gla_packed_call(qk, v, g, la, w, j, batch, seq, s_all, s_stack):
    rows, seg = PACKED_ROWS, seq
    nseg = rows // seg
    tile = lambda width: pl.BlockSpec((rows, width), lambda b: (b, 0))
    s_spec = pl.BlockSpec((None, nseg, N_HEADS, DK, DV), lambda b: (j, b, 0, 0, 0))
    in_specs = [tile(2 * QK_W), tile(V_W), tile(V_W), tile(QK_W), _resident((1, V_W), (j,)),
                _resident((rows, rows)), s_spec]
    args = [qk, v, g, la, w["gla"]["hnorm_w"], _gla_codes(rows, seg), s_all]
    aliases = {}
    if s_stack is None:
        out_spec = pl.BlockSpec((s_all.shape[0], nseg, N_HEADS, DK, DV), lambda b: (0, b, 0, 0, 0))
    else:
        out_spec = s_spec
        in_specs.append(pl.BlockSpec(memory_space=pl.ANY))
        args.append(s_stack)
        aliases = {len(args) - 1: 1}
    return pl.pallas_call(
        functools.partial(_gla_packed_kernel, seg=seg, aliased=bool(aliases)),
        grid=(batch * seq // rows,),
        in_specs=in_specs,
        out_specs=[tile(V_W), out_spec],
        out_shape=[jax.ShapeDtypeStruct((batch * seq, V_W), BF16), jax.ShapeDtypeStruct(s_all.shape, F32)],
        input_output_aliases=aliases,
        compiler_params=pltpu.CompilerParams(dimension_semantics=("parallel",), vmem_limit_bytes=VMEM_LIMIT_BYTES),
        name="gla_packed",
    )(*args)


def _scanpost_kernel(*refs, is_gla, rows, tiles_per_seq, ntiles, final):
    n_scan_w = 2 if is_gla else 4
    n_post_w = 6 if final else 5
    n_state = 1 if is_gla else 4
    tile_refs = refs[0:4]
    scan_w = refs[4:4 + n_scan_w]
    x_ref = refs[4 + n_scan_w]
    wout_ref, nw_ref, wg_ref, wu_ref, wd_ref = refs[5 + n_scan_w:10 + n_scan_w]
    o_ref = refs[5 + n_scan_w + n_post_w]
    state_out = refs[6 + n_scan_w + n_post_w:6 + n_scan_w + n_post_w + n_state]
    scratch = refs[6 + n_scan_w + n_post_w + n_state:]
    gated_ref, h_ref = scratch[0], scratch[1]

    step = pl.program_id(0)
    valid = step < ntiles
    first = jnp.logical_and(valid, lax.rem(step, tiles_per_seq) == 0)

    @pl.when(step == 0)
    def _():
        gated_ref[...] = jnp.zeros(gated_ref.shape, gated_ref.dtype)

    if is_gla:
        _gla_init(first, state_out[0])
    else:
        _mlstm_init(first, state_out, scratch[2])

    def scan_parts():
        if is_gla:
            b2_all = {r0: _gla_prologue(tile_refs[3].at[r0:r0 + GLA_ROWS, :], GLA_ROWS)
                      for r0 in range(0, rows, GLA_ROWS)}
            yield
            for h in range(N_HEADS):
                for r0 in range(0, rows, GLA_ROWS):
                    part_refs = tuple(r.at[r0:r0 + GLA_ROWS, :] for r in tile_refs[:3])
                    yield from _gla_head(h, b2_all[r0][:, h * DK:(h + 1) * DK], part_refs, scan_w[0],
                                         scan_w[1], None, state_out[0], gated_ref.at[r0:r0 + GLA_ROWS, :],
                                         rows=GLA_ROWS, seg=GLA_ROWS, valid=valid)
        else:
            ctx = _mlstm_prologue(tile_refs, scan_w, None, state_out, scratch[2], rows=rows, seg=rows, valid=valid)
            yield
            for h in range(N_HEADS):
                yield from _mlstm_head(h, ctx, tile_refs, scan_w, None, state_out, gated_ref, rows=rows, seg=rows,
                                       valid=valid)

    n_scan = 1 + N_HEADS * (5 * (rows // GLA_ROWS) if is_gla else 4)
    post = {}

    def out_proj():
        post["x2"] = x_ref[...] + jnp.dot(gated_ref[...], wout_ref[...], preferred_element_type=F32)
        post["xn"] = _rms(post["x2"], nw_ref[2:3, :]).astype(BF16)

    def ffn_cols(c):
        g = jnp.dot(post["xn"], wg_ref[:, c:c + FFN_COLS], preferred_element_type=F32)
        u = jnp.dot(post["xn"], wu_ref[:, c:c + FFN_COLS], preferred_element_type=F32)
        h_ref[:, c:c + FFN_COLS] = (_silu(g) * u).astype(BF16)

    def down_proj(lo, hi):
        part = jnp.dot(h_ref[:, lo:hi], wd_ref[lo:hi, :], preferred_element_type=F32)
        post["y"] = part if "y" not in post else post["y"] + part

    post_items = [out_proj]
    n_cols = D_FF // FFN_COLS
    for p in range(N_HEADS):
        lo, hi = FFN_COLS * (p * n_cols // N_HEADS), FFN_COLS * ((p + 1) * n_cols // N_HEADS)
        post_items += [functools.partial(ffn_cols, c) for c in range(lo, hi, FFN_COLS)]
        post_items.append(functools.partial(down_proj, lo, hi))

    lead = 2 if is_gla else 4
    post_items[0]()
    done = 1
    for i, _ in enumerate(scan_parts()):
        upto = lead + i * (len(post_items) - lead) // (n_scan - 1)
        for item in post_items[done:upto]:
            item()
        done = upto
    assert done == len(post_items)

    x3 = post["x2"] + 0.5 * post["y"]
    o_ref[...] = _rms(x3, refs[10 + n_scan_w][...]) if final else x3


def _scanpost_call(x1, qk, v, og, sm, w, l, batch, seq):
    is_gla = l % 2 == 1
    j = l // 2
    final = l == DEPTH - 1
    rows = TOKEN_TILE
    tiles_per_seq = seq // rows
    ntiles = batch * tiles_per_seq
    cur = lambda width: pl.BlockSpec((rows, width), lambda s: (jnp.minimum(s, ntiles - 1), 0))
    prev = pl.BlockSpec((rows, D_MODEL), lambda s: (jnp.maximum(s - 1, 0), 0))
    per_seq = lambda shape: pl.BlockSpec(
        (1,) + shape, lambda s: (jnp.minimum(s, ntiles - 1) // tiles_per_seq,) + (0,) * len(shape))
    if is_gla:
        scan_specs = [_resident((1, V_W), (j,)), _resident((GLA_ROWS, GLA_ROWS))]
        scan_args = [w["gla"]["hnorm_w"], _gla_codes(GLA_ROWS, GLA_ROWS)]
        state_specs = [per_seq((N_HEADS, DK, DV))]
        state_shapes = [jax.ShapeDtypeStruct((batch, N_HEADS, DK, DV), F32)]
        scratch = []
        sm_w = QK_W
    else:
        scan_specs, scan_args = _mlstm_weight_specs(w, j)
        state_specs = [per_seq((CONV_W - 1, 2 * QK_W)), per_seq((N_HEADS, DK, DV)), per_seq((1, QK_W)),
                       per_seq((1, N_HEADS))]
        state_shapes = [jax.ShapeDtypeStruct((batch, CONV_W - 1, 2 * QK_W), F32),
                        jax.ShapeDtypeStruct((batch, N_HEADS, DK, DV), F32),
                        jax.ShapeDtypeStruct((batch, 1, QK_W), F32),
                        jax.ShapeDtypeStruct((batch, 1, N_HEADS), F32)]
        scratch = [pltpu.VMEM((1, rows + SUBLANES, 2 * QK_W), F32)]
        sm_w = LANES
    post_specs, post_args = _post_specs(w, l)
    return pl.pallas_call(
        functools.partial(_scanpost_kernel, is_gla=is_gla, rows=rows, tiles_per_seq=tiles_per_seq, ntiles=ntiles,
                          final=final),
        grid=(ntiles + 1,),
        in_specs=[cur(2 * QK_W), cur(V_W), cur(V_W), cur(sm_w)] + scan_specs + [prev] + post_specs,
        out_specs=[prev] + state_specs,
        out_shape=[jax.ShapeDtypeStruct((batch * seq, D_MODEL), F32)] + state_shapes,
        scratch_shapes=[pltpu.VMEM((rows, V_W), BF16), pltpu.VMEM((rows, D_FF), BF16)] + scratch,
        compiler_params=pltpu.CompilerParams(dimension_semantics=("arbitrary",), vmem_limit_bytes=VMEM_LIMIT_BYTES),
        name=("gla" if is_gla else "mlstm") + "_scanpost",
    )(qk, v, og, sm, *scan_args, x1, *post_args)


def _prepare_weights(norm_w, final_norm_w, ffn_w_gate, ffn_w_up, ffn_w_down,
                     mlstm_w_in, mlstm_conv_w, mlstm_conv_b, mlstm_b_i, mlstm_b_f, mlstm_hnorm_w, mlstm_w_out,
                     gla_w_in, gla_w_a2, gla_b_a, gla_hnorm_w, gla_w_out):
    def small(w_in, n):
        return jnp.pad(w_in[:, :, PROJ_W:], ((0, 0), (0, 0), (0, LANES - n))).astype(BF16)

    gate_b = jnp.pad(jnp.concatenate([mlstm_b_i, mlstm_b_f], axis=1), ((0, 0), (0, LANES - 2 * N_HEADS)))
    return {
        "norm": norm_w, "final": final_norm_w[None, :],
        "gate": _to_bf16(ffn_w_gate.reshape(2 * DEPTH, D_MODEL, D_FF)),
        "up": _to_bf16(ffn_w_up.reshape(2 * DEPTH, D_MODEL, D_FF)),
        "down": _to_bf16(ffn_w_down.reshape(2 * DEPTH, D_FF, D_MODEL)),
        "mlstm": {
            "w_in": _to_bf16(mlstm_w_in), "w_small": small(mlstm_w_in, 2 * N_HEADS),
            "w_out": _to_bf16(mlstm_w_out), "conv_w": mlstm_conv_w, "conv_b": mlstm_conv_b[:, None, :],
            "gate_b": gate_b[:, None, :], "hnorm_w": mlstm_hnorm_w[:, None, :],
        },
        "gla": {
            "w_in": _to_bf16(gla_w_in), "w_small": small(gla_w_in, GATE_RANK), "w_out": _to_bf16(gla_w_out),
            "w_a2": jnp.pad(gla_w_a2, ((0, 0), (0, LANES - GATE_RANK), (0, 0))).astype(BF16),
            "b_a": gla_b_a[:, None, :], "hnorm_w": gla_hnorm_w[:, None, :],
        },
    }


def _prompt_trunk(x, w):
    batch, seq, _ = x.shape
    x = x.reshape(batch * seq, D_MODEL)
    new_c, new_n, new_m, new_conv, new_s = [], [], [], [], []
    for l in range(DEPTH):
        x1, qk, v, og, sm = _pre_call(x, w, l)
        if l % 2 == 1:
            x, s_new = _scanpost_call(x1, qk, v, og, sm, w, l, batch, seq)
            new_s.append(s_new)
        else:
            x, conv_new, c_new, n_new, m_new = _scanpost_call(x1, qk, v, og, sm, w, l, batch, seq)
            new_c.append(c_new); new_conv.append(conv_new)
            new_n.append(n_new.reshape(batch, N_HEADS, DK)); new_m.append(m_new.reshape(batch, N_HEADS))
    return (x.reshape(batch, seq, D_MODEL), jnp.stack(new_c), jnp.stack(new_n), jnp.stack(new_m),
            jnp.stack(new_conv), jnp.stack(new_s))


def _sample_trunk(x, w, c_all, n_all, m_all, conv_all, s_all):
    batch, seq, _ = x.shape
    x = x.reshape(batch * seq, D_MODEL)
    new_n, new_m, new_conv = [], [], []
    c_stack = s_stack = None
    for l in range(DEPTH):
        j = l // 2
        x1, qk, v, og, sm = _pre_call(x, w, l)
        if l % 2 == 1:
            gated, s_stack = _gla_packed_call(qk, v, og, sm, w, j, batch, seq, s_all, s_stack)
        else:
            gated, conv_new, c_stack, n_new, m_new = _mlstm_packed_call(
                qk, v, og, sm, w, j, batch, seq, conv_all, c_all, n_all[j], m_all[j], c_stack)
            new_n.append(n_new); new_m.append(m_new); new_conv.append(conv_new)
        x = _post_call(x1, gated, w, l)
    return (x.reshape(batch, seq, D_MODEL), c_stack, jnp.stack(new_n), jnp.stack(new_m), jnp.stack(new_conv),
            s_stack)


def kernel(x_prompt, x_sample, state_mlstm_C, state_mlstm_n, state_mlstm_m, state_mlstm_conv, state_gla_S, norm_w, final_norm_w, ffn_w_gate, ffn_w_up, ffn_w_down, mlstm_w_in, mlstm_conv_w, mlstm_conv_b, mlstm_b_i, mlstm_b_f, mlstm_hnorm_w, mlstm_w_out, gla_w_in, gla_w_a2, gla_b_a, gla_hnorm_w, gla_w_out):
    w = _prepare_weights(norm_w, final_norm_w, ffn_w_gate, ffn_w_up, ffn_w_down,
                         mlstm_w_in, mlstm_conv_w, mlstm_conv_b, mlstm_b_i, mlstm_b_f, mlstm_hnorm_w, mlstm_w_out,
                         gla_w_in, gla_w_a2, gla_b_a, gla_hnorm_w, gla_w_out)
    y_p, c_p, n_p, m_p, conv_p, s_p = _prompt_trunk(x_prompt, w)
    y_s, c_s, n_s, m_s, conv_s, s_s = _sample_trunk(
        x_sample, w, state_mlstm_C, state_mlstm_n, state_mlstm_m, state_mlstm_conv, state_gla_S)
    return (y_p, y_s, c_p, n_p, m_p, conv_p, s_p, c_s, n_s, m_s, conv_s, s_s)
```

```python
import functools
import math

import jax
import jax.numpy as jnp
import numpy as np
from jax import lax
from jax.experimental import pallas as pl
from jax.experimental.pallas import tpu as pltpu

D_MODEL = 1024
DEPTH = 4
N_HEADS = 4
DK = 128
DV = 256
QK_W = N_HEADS * DK
V_W = N_HEADS * DV
CONV_W = 4
GATE_RANK = 16
GLA_NORMALIZER = 16.0
D_FF = 2816
EPS = 1e-6
PROJ_W = 2 * QK_W + 2 * V_W
LOG2E = math.log2(math.e)

LANES = 128
SUBLANES = 8
VMEM_LIMIT_BYTES = 58 * 1024 * 1024

TOKEN_TILE = 256
PRE_TILE = 512
FFN_COLS = 256
MLSTM_FUSED_TILE = 512
GLA_FUSED_TILE = 256
MLSTM_ROWS = 256
GLA_ROWS = 128
PACKED_ROWS = 128

BF16 = jnp.bfloat16
F32 = jnp.float32


def _resident(shape, index=None):
    index = tuple(index or ())
    block = (None,) * len(index) + tuple(shape)
    return pl.BlockSpec(block, lambda *_: index + (0,) * len(shape), pipeline_mode=pl.Buffered(1))


def _rms(x, w):
    return x * lax.rsqrt(jnp.mean(x * x, axis=-1, keepdims=True) + EPS) * w


def _silu(x):
    return x * jax.nn.sigmoid(x)


def _log_sigmoid(x):
    return jnp.minimum(x, 0.0) - jnp.log1p(jnp.exp(-jnp.abs(x)))


def _dot(a, b):
    return jnp.dot(a.astype(BF16), b.astype(BF16), preferred_element_type=F32)


def _dot_nt(a, b):
    return lax.dot_general(a.astype(BF16), b.astype(BF16), (((1,), (1,)), ((), ())), preferred_element_type=F32)


def _dot_tn(a, b):
    return lax.dot_general(a.astype(BF16), b.astype(BF16), (((0,), (0,)), ((), ())), preferred_element_type=F32)


def _seg_cumsum(x, seg):
    n = x.shape[0]
    row = lax.broadcasted_iota(jnp.int32, (n, n), 0)
    col = lax.broadcasted_iota(jnp.int32, (n, n), 1)
    shift = seg.bit_length() - 1
    tri = jnp.where((col <= row) & ((row >> shift) == (col >> shift)), 1.0, 0.0).astype(BF16)
    hi = x.astype(BF16)
    r1 = x - hi.astype(F32)
    mid = r1.astype(BF16)
    lo = (r1 - mid.astype(F32)).astype(BF16)
    return jnp.dot(jnp.concatenate([tri, tri, tri], axis=1), jnp.concatenate([hi, mid, lo], axis=0),
                   preferred_element_type=F32)


def _group_row(x, group, idx):
    n, c = x.shape
    if n == group:
        return x[idx:idx + 1, :]
    return jnp.concatenate(
        [jnp.broadcast_to(x[g * group + idx:g * group + idx + 1, :], (group, c)) for g in range(n // group)], axis=0)


def _group_max(x, group):
    n, c = x.shape
    if n == group:
        return jnp.max(x, axis=0, keepdims=True)
    return jnp.concatenate(
        [jnp.broadcast_to(jnp.max(x[g * group:(g + 1) * group, :], axis=0, keepdims=True), (group, c))
         for g in range(n // group)], axis=0)


def _col_to_row(col):
    n = col.shape[0]
    eye = lax.broadcasted_iota(jnp.int32, (n, n), 0) == lax.broadcasted_iota(jnp.int32, (n, n), 1)
    return jnp.sum(jnp.where(eye, col, 0.0), axis=0, keepdims=True)


def _row_to_col(row):
    n = row.shape[1]
    eye = lax.broadcasted_iota(jnp.int32, (n, n), 0) == lax.broadcasted_iota(jnp.int32, (n, n), 1)
    return jnp.sum(jnp.where(eye, row, 0.0), axis=1, keepdims=True)


def _keep(valid, new, old_ref_value):
    return new if valid is None else jnp.where(valid, new, old_ref_value)


def _cast_kernel(w_ref, o_ref):
    o_ref[...] = w_ref[...].astype(o_ref.dtype)


def _to_bf16(w, first=0, count=None):
    n, r, c = w.shape
    count = n - first if count is None else count
    return pl.pallas_call(
        _cast_kernel, grid=(count,),
        in_specs=[pl.BlockSpec((1, r, c), lambda i: (first + i, 0, 0))],
        out_specs=pl.BlockSpec((1, r, c), lambda i: (i, 0, 0)),
        out_shape=jax.ShapeDtypeStruct((count, r, c), BF16),
        compiler_params=pltpu.CompilerParams(dimension_semantics=("parallel",), vmem_limit_bytes=VMEM_LIMIT_BYTES),
        name="cast_bf16",
    )(w)


def _ffn(x, nw, wg_ref, wu_ref, wd_ref, h_ref):
    xn = _rms(x, nw).astype(BF16)
    for c in range(0, D_FF, FFN_COLS):
        g = jnp.dot(xn, wg_ref[:, c:c + FFN_COLS], preferred_element_type=F32)
        u = jnp.dot(xn, wu_ref[:, c:c + FFN_COLS], preferred_element_type=F32)
        h_ref[:, c:c + FFN_COLS] = (_silu(g) * u).astype(BF16)
    return x + 0.5 * jnp.dot(h_ref[...], wd_ref[...], preferred_element_type=F32)


def _pre_kernel(*refs, is_gla, cast_ahead):
    n_in = 9 if is_gla else 7
    if cast_ahead:
        for src, dst in zip(refs[n_in:n_in + 3], refs[n_in + 8:n_in + 11]):
            dst[...] = src[...].astype(dst.dtype)
        refs = refs[:n_in] + refs[n_in + 3:n_in + 8] + refs[-1:]
    if is_gla:
        (x_ref, nw_ref, wg_ref, wu_ref, wd_ref, win_ref, wsm_ref, wa2_ref, ba_ref,
         x1_ref, qk_ref, v_ref, og_ref, sm_ref, h_ref) = refs
    else:
        (x_ref, nw_ref, wg_ref, wu_ref, wd_ref, win_ref, wsm_ref,
         x1_ref, qk_ref, v_ref, og_ref, sm_ref, h_ref) = refs
    x1 = _ffn(x_ref[...], nw_ref[0:1, :], wg_ref, wu_ref, wd_ref, h_ref)
    x1_ref[...] = x1
    xn = _rms(x1, nw_ref[1:2, :]).astype(BF16)
    small = jnp.dot(xn, wsm_ref[...], preferred_element_type=F32)
    if is_gla:
        a = jnp.dot(small.astype(BF16), wa2_ref[...], preferred_element_type=F32) + ba_ref[...]
        sm_ref[...] = _log_sigmoid(a) * (1.0 / GLA_NORMALIZER)
    else:
        sm_ref[...] = small
    for i, p_ref in enumerate((qk_ref, v_ref, og_ref)):
        p_ref[...] = jnp.dot(xn, win_ref[:, i * D_MODEL:(i + 1) * D_MODEL],
                             preferred_element_type=F32).astype(p_ref.dtype)


def _pre_call(x, w, l, cast_ahead=False):
    m = x.shape[0]
    is_gla = l % 2 == 1
    j = l // 2
    steps = m // PRE_TILE
    tile = lambda width: pl.BlockSpec((PRE_TILE, width), lambda i: (i, 0))
    sm_w = QK_W if is_gla else LANES
    mix = w["gla"] if is_gla else w["mlstm"]
    in_specs = [tile(D_MODEL), _resident((3, D_MODEL), (l,)), _resident((D_MODEL, D_FF), (0,)),
                _resident((D_MODEL, D_FF), (0,)), _resident((D_FF, D_MODEL), (0,)),
                _resident((D_MODEL, PROJ_W), (j,)), _resident((D_MODEL, LANES), (j,))]
    args = [x, w["norm"], w["gate"][l], w["up"][l], w["down"][l], mix["w_in"], mix["w_small"]]
    if is_gla:
        in_specs += [_resident((LANES, QK_W), (j,)), _resident((1, QK_W), (j,))]
        args += [mix["w_a2"], mix["b_a"]]
    out_specs = [tile(D_MODEL), tile(D_MODEL), tile(D_MODEL), tile(D_MODEL), tile(sm_w)]
    out_shape = [jax.ShapeDtypeStruct((m, D_MODEL), F32), jax.ShapeDtypeStruct((m, D_MODEL), F32),
                 jax.ShapeDtypeStruct((m, D_MODEL), BF16), jax.ShapeDtypeStruct((m, D_MODEL), F32),
                 jax.ShapeDtypeStruct((m, sm_w), F32)]
    if cast_ahead:
        for src in w["ffn_f32"]:
            rows, cols = src.shape[0] // DEPTH, src.shape[1]
            slab = rows // steps
            in_specs.append(pl.BlockSpec((slab, cols), lambda i, first=(l + 1) * steps: (first + i, 0)))
            args.append(src)
            out_specs.append(pl.BlockSpec((slab, cols), lambda i: (i, 0)))
            out_shape.append(jax.ShapeDtypeStruct((rows, cols), BF16))
    outs = pl.pallas_call(
        functools.partial(_pre_kernel, is_gla=is_gla, cast_ahead=cast_ahead),
        grid=(steps,),
        in_specs=in_specs,
        out_specs=out_specs,
        out_shape=out_shape,
        scratch_shapes=[pltpu.VMEM((PRE_TILE, D_FF), BF16)],
        compiler_params=pltpu.CompilerParams(dimension_semantics=("parallel",), vmem_limit_bytes=VMEM_LIMIT_BYTES),
        name="pre_gla" if is_gla else "pre_mlstm",
    )(*args)
    if cast_ahead:
        for name, cast in zip(("gate", "up", "down"), outs[5:]):
            w[name][l + 1] = cast.reshape((2, -1, cast.shape[1]))
    return outs[:5]


def _post(x1, gated, post_refs, h_ref, final):
    wout_ref, nw_ref, wg_ref, wu_ref, wd_ref = post_refs[:5]
    x2 = x1 + jnp.dot(gated, wout_ref[...], preferred_element_type=F32)
    x3 = _ffn(x2, nw_ref[2:3, :], wg_ref, wu_ref, wd_ref, h_ref)
    return _rms(x3, post_refs[5][...]) if final else x3


def _post_specs(w, l):
    is_gla = l % 2 == 1
    j = l // 2
    mix = w["gla"] if is_gla else w["mlstm"]
    specs = [_resident((V_W, D_MODEL), (j,)), _resident((3, D_MODEL), (l,)), _resident((D_MODEL, D_FF), (1,)),
             _resident((D_MODEL, D_FF), (1,)), _resident((D_FF, D_MODEL), (1,))]
    args = [mix["w_out"], w["norm"], w["gate"][l], w["up"][l], w["down"][l]]
    if l == DEPTH - 1:
        specs.append(_resident((1, D_MODEL)))
        args.append(w["final"])
    return specs, args


def _post_kernel(*refs, final):
    x_ref, gated_ref = refs[:2]
    o_ref, h_ref = refs[-2:]
    o_ref[...] = _post(x_ref[...], gated_ref[...], refs[2:-2], h_ref, final)


def _post_call(x1, gated, w, l):
    m = x1.shape[0]
    final = l == DEPTH - 1
    tile = pl.BlockSpec((TOKEN_TILE, D_MODEL), lambda i: (i, 0))
    specs, args = _post_specs(w, l)
    return pl.pallas_call(
        functools.partial(_post_kernel, final=final),
        grid=(m // TOKEN_TILE,),
        in_specs=[tile, tile] + specs,
        out_specs=tile,
        out_shape=jax.ShapeDtypeStruct((m, D_MODEL), F32),
        scratch_shapes=[pltpu.VMEM((TOKEN_TILE, D_FF), BF16)],
        compiler_params=pltpu.CompilerParams(dimension_semantics=("parallel",), vmem_limit_bytes=VMEM_LIMIT_BYTES),
        name="post_final" if final else "post",
    )(x1, gated, *args)


def _mlstm_init(first, state_out, ext_ref):
    conv_out, c_ref, n_ref, m_ref = state_out

    @pl.when(first)
    def _():
        ext_ref[0, 0:SUBLANES, :] = jnp.zeros((SUBLANES, 2 * QK_W), F32)
        conv_out[...] = jnp.zeros(conv_out.shape, F32)
        c_ref[...] = jnp.zeros(c_ref.shape, F32)
        n_ref[...] = jnp.zeros(n_ref.shape, F32)
        m_ref[...] = jnp.zeros(m_ref.shape, F32)


def _mlstm_prologue(tile_refs, w_refs, state_in, state_out, ext_ref, *, rows, seg, chunk=None, valid=None):
    qk_ref, gt_ref = tile_refs[0], tile_refs[3]
    cw_ref, cb_ref, gb_ref = w_refs[:3]
    conv_out = state_out[0]
    carry = state_in is None
    nseg = rows // seg
    hist = SUBLANES

    pieces = []
    for i in range(nseg):
        if not carry:
            ext_ref[i, hist - (CONV_W - 1):hist, :] = state_in[0][i]
        ext_ref[i, hist:hist + seg, :] = qk_ref[i * seg:(i + 1) * seg, :]
        acc = cb_ref[...] + cw_ref[CONV_W - 1:CONV_W, :] * ext_ref[i, hist:hist + seg, :]
        for j in range(1, CONV_W):
            acc = acc + cw_ref[CONV_W - 1 - j:CONV_W - j, :] * ext_ref[i, hist - j:hist - j + seg, :]
        conv_out[i] = _keep(valid, ext_ref[i, hist + seg - (CONV_W - 1):hist + seg, :], conv_out[i])
        pieces.append(acc)
    if carry:
        ext_ref[0, 0:hist, :] = ext_ref[0, seg:seg + hist, :]
    qk = _silu(pieces[0] if nseg == 1 else jnp.concatenate(pieces, axis=0))

    gt = gt_ref[...] + gb_ref[...]
    k = qk[:, QK_W:] * (DK ** -0.5)
    lf = _log_sigmoid(gt)
    if chunk is None or chunk == seg:
        bcum = _seg_cumsum(lf, seg)
    else:
        bcum = jnp.concatenate([_seg_cumsum(lf[r0:r0 + chunk, :], chunk) for r0 in range(0, rows, chunk)], axis=0)
    return qk[:, :QK_W], k, qk[:, :QK_W].astype(BF16), k.astype(BF16), gt, bcum


def _mlstm_head(h, ctx, tile_refs, w_refs, state_in, state_out, out_ref, *, rows, seg, valid=None):
    q_all, k_all, qb_all, kb_all, gt, bcum = ctx
    v_ref, o_ref = tile_refs[1], tile_refs[2]
    hw_ref = w_refs[3]
    _, c_ref, n_ref, m_ref = state_out
    carry = state_in is None
    nseg = rows // seg
    row = lax.broadcasted_iota(jnp.int32, (rows, rows), 0)
    col = lax.broadcasted_iota(jnp.int32, (rows, rows), 1)
    shift = seg.bit_length() - 1
    causal = (col <= row) & ((row >> shift) == (col >> shift))

    q = q_all[:, h * DK:(h + 1) * DK]
    k = k_all[:, h * DK:(h + 1) * DK]
    qb = qb_all[:, h * DK:(h + 1) * DK]
    v = v_ref[:, h * DV:(h + 1) * DV]
    ic = gt[:, h:h + 1]
    b = bcum[:, N_HEADS + h:N_HEADS + h + 1]
    if carry:
        m_prev = m_ref[0, :, h:h + 1]
        n_prev = n_ref[0, :, h * DK:(h + 1) * DK]
    else:
        m_prev = state_in[3][:, h:h + 1]
        n_prev = state_in[2][:, h * DK:(h + 1) * DK]
    c_prev = [c_ref[0, h] if carry else state_in[1][i, h] for i in range(nseg)]

    qk_t = _dot_nt(qb, kb_all[:, h * DK:(h + 1) * DK])
    inter = [_dot(qb[i * seg:(i + 1) * seg, :], c_prev[i]) for i in range(nseg)]
    inter = inter[0] if nseg == 1 else jnp.concatenate(inter, axis=0)
    a_row = _col_to_row(ic - b)
    dm = jnp.where(causal, b + a_row, -jnp.inf)
    mt = jnp.maximum(b + m_prev, jnp.max(dm, axis=1, keepdims=True))
    w_inter = jnp.exp(b + m_prev - mt)
    p = jnp.exp(dm - mt)
    b_last = _group_row(b, seg, seg - 1)
    g = b_last - b + ic
    m_new = jnp.maximum(b_last + m_prev, _group_max(g, seg))
    sc_prev = jnp.exp(b_last + m_prev - m_new)
    kw = k * jnp.exp(g - m_new)
    kw_b = kw.astype(BF16)
    yield

    s = qk_t * p
    s_b = s.astype(BF16)
    yield

    intra = _dot(s_b, v)
    kv = [_dot_tn(kw_b[i * seg:(i + 1) * seg, :], v[i * seg:(i + 1) * seg, :]) for i in range(nseg)]
    yield

    num = intra + w_inter * inter
    den = jnp.sum(s, axis=1, keepdims=True) + w_inter * jnp.sum(q * n_prev, axis=1, keepdims=True)
    hh = num / jnp.maximum(jnp.abs(den), jnp.exp(-mt))
    for i in range(nseg):
        lo, hi = i * seg, (i + 1) * seg
        sc_i = sc_prev[lo:lo + 1, :]
        c_ref[i, h] = _keep(valid, sc_i * c_prev[i] + kv[i], c_prev[i])
        n_old = n_prev[lo:lo + 1, :]
        n_ref[i, :, h * DK:(h + 1) * DK] = _keep(
            valid, sc_i * n_old + jnp.sum(kw[lo:hi, :], axis=0, keepdims=True), n_old)
        m_ref[i, :, h:h + 1] = _keep(valid, m_new[lo:lo + 1, :], m_prev[lo:lo + 1, :])
    hn = _rms(hh, hw_ref[:, h * DV:(h + 1) * DV])
    out_ref[:, h * DV:(h + 1) * DV] = (hn * jax.nn.sigmoid(o_ref[:, h * DV:(h + 1) * DV])).astype(out_ref.dtype)
    yield


def _mlstm_weight_specs(w, j):
    mw = w["mlstm"]
    specs = [_resident((CONV_W, 2 * QK_W), (j,)), _resident((1, 2 * QK_W), (j,)), _resident((1, LANES), (j,)),
             _resident((1, V_W), (j,))]
    return specs, [mw["conv_w"], mw["conv_b"], mw["gate_b"], mw["hnorm_w"]]


def _mlstm_packed_kernel(*refs, seg, aliased):
    outs = list(refs[13:] if aliased else refs[12:])
    if not aliased:
        stack_ref = outs[2]
        stack_ref[1:] = jnp.zeros((stack_ref.shape[0] - 1,) + stack_ref.shape[1:], F32)
        outs[2] = stack_ref.at[0]
    ctx = _mlstm_prologue(refs[0:4], refs[4:8], refs[8:12], outs[1:5], outs[5], rows=PACKED_ROWS, seg=seg)
    for h in range(N_HEADS):
        for _ in _mlstm_head(h, ctx, refs[0:4], refs[4:8], refs[8:12], outs[1:5], outs[0], rows=PACKED_ROWS, seg=seg):
            pass


def _mlstm_packed_call(qk, v, o, gates, w, j, batch, seq, conv_all, c_all, n0, m0, c_stack):
    rows, seg = PACKED_ROWS, seq
    nseg = rows // seg
    tile = lambda width: pl.BlockSpec((rows, width), lambda b: (b, 0))
    per_seq = lambda shape: pl.BlockSpec((nseg,) + shape, lambda b: (b,) + (0,) * len(shape))
    layer_seq = lambda shape: pl.BlockSpec((None, nseg) + shape, lambda b: (j, b) + (0,) * len(shape))
    w_specs, w_args = _mlstm_weight_specs(w, j)
    in_specs = ([tile(2 * QK_W), tile(V_W), tile(V_W), tile(LANES)] + w_specs
                + [layer_seq((CONV_W - 1, 2 * QK_W)), layer_seq((N_HEADS, DK, DV)), tile(QK_W), tile(N_HEADS)])
    args = [qk, v, o, gates] + w_args + [conv_all, c_all, jnp.repeat(n0.reshape(batch, QK_W), seq, axis=0),
                                         jnp.repeat(m0, seq, axis=0)]
    aliases = {}
    if c_stack is None:
        c_spec = pl.BlockSpec((c_all.shape[0], nseg, N_HEADS, DK, DV), lambda b: (0, b, 0, 0, 0))
    else:
        c_spec = layer_seq((N_HEADS, DK, DV))
        in_specs.append(pl.BlockSpec(memory_space=pl.ANY))
        args.append(c_stack)
        aliases = {len(args) - 1: 2}
    out, conv_new, c_stack, n_new, m_new = pl.pallas_call(
        functools.partial(_mlstm_packed_kernel, seg=seg, aliased=bool(aliases)),
        grid=(batch * seq // rows,),
        in_specs=in_specs,
        out_specs=[tile(V_W), per_seq((CONV_W - 1, 2 * QK_W)), c_spec, per_seq((1, QK_W)), per_seq((1, N_HEADS))],
        out_shape=[jax.ShapeDtypeStruct((batch * seq, V_W), BF16),
                   jax.ShapeDtypeStruct((batch, CONV_W - 1, 2 * QK_W), F32),
                   jax.ShapeDtypeStruct(c_all.shape, F32),
                   jax.ShapeDtypeStruct((batch, 1, QK_W), F32),
                   jax.ShapeDtypeStruct((batch, 1, N_HEADS), F32)],
        scratch_shapes=[pltpu.VMEM((nseg, seg + SUBLANES, 2 * QK_W), F32)],
        input_output_aliases=aliases,
        compiler_params=pltpu.CompilerParams(dimension_semantics=("parallel",), vmem_limit_bytes=VMEM_LIMIT_BYTES),
        name="mlstm_packed",
    )(*args)
    return out, conv_new, c_stack, n_new.reshape(batch, N_HEADS, DK), m_new.reshape(batch, N_HEADS)


def _gla_init(first, s_ref):
    @pl.when(first)
    def _():
        s_ref[...] = jnp.zeros(s_ref.shape, F32)


def _gla_halves(seg):
    halves, half = [], seg // 2
    while half >= SUBLANES:
        halves.append(half)
        half //= 2
    return halves


def _gla_codes(rows, seg):
    sub = min(SUBLANES, seg)
    r = np.arange(rows)[:, None]
    c = np.arange(rows)[None, :]
    code = np.full((rows, rows), -1, np.int32)
    halves = _gla_halves(seg)
    for level, half in enumerate(halves):
        grp = 2 * half
        code[(r // grp == c // grp) & (r % grp >= half) & (c % grp < half)] = level
    inside = (r // sub == c // sub) & (c <= r)
    return np.where(inside, len(halves) + (c % sub), code).astype(np.int32)


def _gla_prologue(la_ref, seg):
    return _seg_cumsum(la_ref[...], seg) * LOG2E


def _gla_head(h, b2, tile_refs, hw_ref, code_ref, s_in, s_ref, out_ref, *, rows, seg, valid=None):
    qk_ref, v_ref, g_ref = tile_refs[:3]
    carry = s_in is None
    nseg = rows // seg
    sub = min(SUBLANES, seg)
    halves = _gla_halves(seg)
    code = code_ref[...]
    rowk = lax.broadcasted_iota(jnp.int32, (rows, DK), 0)

    q = qk_ref[:, h * DK:(h + 1) * DK] * (DK ** -0.5)
    k = qk_ref[:, QK_W + h * DK:QK_W + (h + 1) * DK]
    v = v_ref[:, h * DV:(h + 1) * DV]
    s_prev = [s_ref[0, h] if carry else s_in[i, h] for i in range(nseg)]

    qd = (q * jnp.exp2(b2)).astype(BF16)
    s_prev_b = [s.astype(BF16) for s in s_prev]
    xs = []
    for half in halves:
        second = (rowk & (2 * half - 1)) >= half
        ref = _group_row(b2, 2 * half, half - 1)
        xs.append((jnp.where(second, q, k) * jnp.exp2(-jnp.abs(b2 - ref))).astype(BF16))
    yield

    if nseg > 1:
        inter = jnp.concatenate([_dot(qd[i * seg:(i + 1) * seg, :], s_prev_b[i]) for i in range(nseg)], axis=0)
    cross = [_dot_nt(x, x) for x in xs]
    wgts = []
    for s in range(sub):
        dec = jnp.exp2(b2 - _group_row(b2, sub, s))
        wgts.append(jnp.sum(q * _group_row(k, sub, s) * dec, axis=1, keepdims=True))
    b_last = _group_row(b2, seg, seg - 1)
    k_dec = (k * jnp.exp2(b_last - b2)).astype(BF16)
    yield

    att = jnp.zeros((rows, rows), F32)
    for level, prod in enumerate(cross):
        att = jnp.where(code == level, prod, att)
    for s, wgt in enumerate(wgts):
        att = jnp.where(code == len(halves) + s, wgt, att)
    att = att.astype(BF16)
    yield

    if nseg > 1:
        o = inter + _dot(att, v)
    else:
        o = jnp.dot(jnp.concatenate([qd, att], axis=1), jnp.concatenate([s_prev_b[0], v], axis=0),
                    preferred_element_type=F32)
    kv = [_dot_tn(k_dec[i * seg:(i + 1) * seg, :], v[i * seg:(i + 1) * seg, :]) for i in range(nseg)]
    yield

    for i in range(nseg):
        decay = _row_to_col(jnp.exp2(b_last[i * seg:i * seg + 1, :]))
        s_ref[i, h] = _keep(valid, decay * s_prev[i] + kv[i], s_prev[i])
    on = _rms(o, hw_ref[:, h * DV:(h + 1) * DV])
    out_ref[:, h * DV:(h + 1) * DV] = (on * _silu(g_ref[:, h * DV:(h + 1) * DV])).astype(out_ref.dtype)
    yield


def _gla_packed_kernel(qk_ref, v_ref, g_ref, la_ref, hw_ref, code_ref, s_in, *rest, seg, aliased):
    out_ref, s_ref = rest[-2:]
    if not aliased:
        s_ref[1:] = jnp.zeros((s_ref.shape[0] - 1,) + s_ref.shape[1:], F32)
        s_ref = s_ref.at[0]
    b2_all = _gla_prologue(la_ref, seg)
    for h in range(N_HEADS):
        for _ in _gla_head(h, b2_all[:, h * DK:(h + 1) * DK], (qk_ref, v_ref, g_ref), hw_ref, code_ref, s_in, s_ref,
                           out_ref, rows=PACKED_ROWS, seg=seg):
            pass


def _gla_packed_call(qk, v, g, la, w, j, batch, seq, s_all, s_stack):
    rows, seg = PACKED_ROWS, seq
    nseg = rows // seg
    tile = lambda width: pl.BlockSpec((rows, width), lambda b: (b, 0))
    s_spec = pl.BlockSpec((None, nseg, N_HEADS, DK, DV), lambda b: (j, b, 0, 0, 0))
    in_specs = [tile(2 * QK_W), tile(V_W), tile(V_W), tile(QK_W), _resident((1, V_W), (j,)),
                _resident((rows, rows)), s_spec]
    args = [qk, v, g, la, w["gla"]["hnorm_w"], _gla_codes(rows, seg), s_all]
    aliases = {}
    if s_stack is None:
        out_spec = pl.BlockSpec((s_all.shape[0], nseg, N_HEADS, DK, DV), lambda b: (0, b, 0, 0, 0))
    else:
        out_spec = s_spec
        in_specs.append(pl.BlockSpec(memory_space=pl.ANY))
        args.append(s_stack)
        aliases = {len(args) - 1: 1}
    return pl.pallas_call(
        functools.partial(_gla_packed_kernel, seg=seg, aliased=bool(aliases)),
        grid=(batch * seq // rows,),
        in_specs=in_specs,
        out_specs=[tile(V_W), out_spec],
        out_shape=[jax.ShapeDtypeStruct((batch * seq, V_W), BF16), jax.ShapeDtypeStruct(s_all.shape, F32)],
        input_output_aliases=aliases,
        compiler_params=pltpu.CompilerParams(dimension_semantics=("parallel",), vmem_limit_bytes=VMEM_LIMIT_BYTES),
        name="gla_packed",
    )(*args)


def _scanpost_kernel(*refs, is_gla, rows, tiles_per_seq, ntiles, final):
    n_scan_w = 2 if is_gla else 4
    n_post_w = 6 if final else 5
    n_state = 1 if is_gla else 4
    tile_refs = refs[0:4]
    scan_w = refs[4:4 + n_scan_w]
    x_ref = refs[4 + n_scan_w]
    wout_ref, nw_ref, wg_ref, wu_ref, wd_ref = refs[5 + n_scan_w:10 + n_scan_w]
    o_ref = refs[5 + n_scan_w + n_post_w]
    state_out = refs[6 + n_scan_w + n_post_w:6 + n_scan_w + n_post_w + n_state]
    scratch = refs[6 + n_scan_w + n_post_w + n_state:]
    gated_ref, h_ref = scratch[0], scratch[1]

    step = pl.program_id(0)
    valid = step < ntiles
    first = jnp.logical_and(valid, lax.rem(step, tiles_per_seq) == 0)

    @pl.when(step == 0)
    def _():
        gated_ref[...] = jnp.zeros(gated_ref.shape, gated_ref.dtype)

    if is_gla:
        _gla_init(first, state_out[0])
    else:
        _mlstm_init(first, state_out, scratch[2])

    def scan_parts():
        if is_gla:
            b2_all = {r0: _gla_prologue(tile_refs[3].at[r0:r0 + GLA_ROWS, :], GLA_ROWS)
                      for r0 in range(0, rows, GLA_ROWS)}
            yield
            for h in range(N_HEADS):
                for r0 in range(0, rows, GLA_ROWS):
                    part_refs = tuple(r.at[r0:r0 + GLA_ROWS, :] for r in tile_refs[:3])
                    yield from _gla_head(h, b2_all[r0][:, h * DK:(h + 1) * DK], part_refs, scan_w[0],
                                         scan_w[1], None, state_out[0], gated_ref.at[r0:r0 + GLA_ROWS, :],
                                         rows=GLA_ROWS, seg=GLA_ROWS, valid=valid)
        else:
            ctx = _mlstm_prologue(tile_refs, scan_w, None, state_out, scratch[2], rows=rows, seg=rows,
                                  chunk=MLSTM_ROWS, valid=valid)
            yield
            for r0 in range(0, rows, MLSTM_ROWS):
                part_ctx = tuple(a[r0:r0 + MLSTM_ROWS, :] for a in ctx)
                part_refs = tuple(r.at[r0:r0 + MLSTM_ROWS, :] for r in tile_refs)
                for h in range(N_HEADS):
                    yield from _mlstm_head(h, part_ctx, part_refs, scan_w, None, state_out,
                                           gated_ref.at[r0:r0 + MLSTM_ROWS, :], rows=MLSTM_ROWS, seg=MLSTM_ROWS,
                                           valid=valid)

    n_scan = 1 + N_HEADS * (5 * (rows // GLA_ROWS) if is_gla else 4 * (rows // MLSTM_ROWS))
    post = {}

    def out_proj():
        post["x2"] = x_ref[...] + jnp.dot(gated_ref[...], wout_ref[...], preferred_element_type=F32)
        post["xn"] = _rms(post["x2"], nw_ref[2:3, :]).astype(BF16)

    def ffn_cols(c):
        g = jnp.dot(post["xn"], wg_ref[:, c:c + FFN_COLS], preferred_element_type=F32)
        u = jnp.dot(post["xn"], wu_ref[:, c:c + FFN_COLS], preferred_element_type=F32)
        h_ref[:, c:c + FFN_COLS] = (_silu(g) * u).astype(BF16)

    def down_proj(lo, hi):
        part = jnp.dot(h_ref[:, lo:hi], wd_ref[lo:hi, :], preferred_element_type=F32)
        post["y"] = part if "y" not in post else post["y"] + part

    post_items = [out_proj]
    n_cols = D_FF // FFN_COLS
    for p in range(N_HEADS):
        lo, hi = FFN_COLS * (p * n_cols // N_HEADS), FFN_COLS * ((p + 1) * n_cols // N_HEADS)
        post_items += [functools.partial(ffn_cols, c) for c in range(lo, hi, FFN_COLS)]
        post_items.append(functools.partial(down_proj, lo, hi))

    lead = 2 if is_gla else 5
    post_items[0]()
    done = 1
    for i, _ in enumerate(scan_parts()):
        upto = lead + i * (len(post_items) - lead) // (n_scan - 1)
        for item in post_items[done:upto]:
            item()
        done = upto
    assert done == len(post_items)

    x3 = post["x2"] + 0.5 * post["y"]
    o_ref[...] = _rms(x3, refs[10 + n_scan_w][...]) if final else x3


def _scanpost_call(x1, qk, v, og, sm, w, l, batch, seq):
    is_gla = l % 2 == 1
    j = l // 2
    final = l == DEPTH - 1
    rows = GLA_FUSED_TILE if is_gla else MLSTM_FUSED_TILE
    tiles_per_seq = seq // rows
    ntiles = batch * tiles_per_seq
    cur = lambda width: pl.BlockSpec((rows, width), lambda s: (jnp.minimum(s, ntiles - 1), 0))
    prev = pl.BlockSpec((rows, D_MODEL), lambda s: (jnp.maximum(s - 1, 0), 0))
    per_seq = lambda shape: pl.BlockSpec(
        (1,) + shape, lambda s: (jnp.minimum(s, ntiles - 1) // tiles_per_seq,) + (0,) * len(shape))
    if is_gla:
        scan_specs = [_resident((1, V_W), (j,)), _resident((GLA_ROWS, GLA_ROWS))]
        scan_args = [w["gla"]["hnorm_w"], _gla_codes(GLA_ROWS, GLA_ROWS)]
        state_specs = [per_seq((N_HEADS, DK, DV))]
        state_shapes = [jax.ShapeDtypeStruct((batch, N_HEADS, DK, DV), F32)]
        scratch = []
        sm_w = QK_W
    else:
        scan_specs, scan_args = _mlstm_weight_specs(w, j)
        state_specs = [per_seq((CONV_W - 1, 2 * QK_W)), per_seq((N_HEADS, DK, DV)), per_seq((1, QK_W)),
                       per_seq((1, N_HEADS))]
        state_shapes = [jax.ShapeDtypeStruct((batch, CONV_W - 1, 2 * QK_W), F32),
                        jax.ShapeDtypeStruct((batch, N_HEADS, DK, DV), F32),
                        jax.ShapeDtypeStruct((batch, 1, QK_W), F32),
                        jax.ShapeDtypeStruct((batch, 1, N_HEADS), F32)]
        scratch = [pltpu.VMEM((1, rows + SUBLANES, 2 * QK_W), F32)]
        sm_w = LANES
    post_specs, post_args = _post_specs(w, l)
    return pl.pallas_call(
        functools.partial(_scanpost_kernel, is_gla=is_gla, rows=rows, tiles_per_seq=tiles_per_seq, ntiles=ntiles,
                          final=final),
        grid=(ntiles + 1,),
        in_specs=[cur(2 * QK_W), cur(V_W), cur(V_W), cur(sm_w)] + scan_specs + [prev] + post_specs,
        out_specs=[prev] + state_specs,
        out_shape=[jax.ShapeDtypeStruct((batch * seq, D_MODEL), F32)] + state_shapes,
        scratch_shapes=[pltpu.VMEM((rows, V_W), BF16), pltpu.VMEM((rows, D_FF), BF16)] + scratch,
        compiler_params=pltpu.CompilerParams(dimension_semantics=("arbitrary",), vmem_limit_bytes=VMEM_LIMIT_BYTES),
        name=("gla" if is_gla else "mlstm") + "_scanpost",
    )(qk, v, og, sm, *scan_args, x1, *post_args)


def _prepare_weights(norm_w, final_norm_w, ffn_w_gate, ffn_w_up, ffn_w_down,
                     mlstm_w_in, mlstm_conv_w, mlstm_conv_b, mlstm_b_i, mlstm_b_f, mlstm_hnorm_w, mlstm_w_out,
                     gla_w_in, gla_w_a2, gla_b_a, gla_hnorm_w, gla_w_out):
    def small(w_in, n):
        return jnp.pad(w_in[:, :, PROJ_W:], ((0, 0), (0, 0), (0, LANES - n))).astype(BF16)

    gate_b = jnp.pad(jnp.concatenate([mlstm_b_i, mlstm_b_f], axis=1), ((0, 0), (0, LANES - 2 * N_HEADS)))
    return {
        "norm": norm_w, "final": final_norm_w[None, :],
        "gate": [_to_bf16(ffn_w_gate.reshape(2 * DEPTH, D_MODEL, D_FF), 0, 2)] + [None] * (DEPTH - 1),
        "up": [_to_bf16(ffn_w_up.reshape(2 * DEPTH, D_MODEL, D_FF), 0, 2)] + [None] * (DEPTH - 1),
        "down": [_to_bf16(ffn_w_down.reshape(2 * DEPTH, D_FF, D_MODEL), 0, 2)] + [None] * (DEPTH - 1),
        "ffn_f32": [ffn_w_gate.reshape(-1, D_FF), ffn_w_up.reshape(-1, D_FF), ffn_w_down.reshape(-1, D_MODEL)],
        "mlstm": {
            "w_in": mlstm_w_in.astype(BF16)[:, :, :PROJ_W], "w_small": small(mlstm_w_in, 2 * N_HEADS),
            "w_out": _to_bf16(mlstm_w_out), "conv_w": mlstm_conv_w, "conv_b": mlstm_conv_b[:, None, :],
            "gate_b": gate_b[:, None, :], "hnorm_w": mlstm_hnorm_w[:, None, :],
        },
        "gla": {
            "w_in": gla_w_in.astype(BF16)[:, :, :PROJ_W], "w_small": small(gla_w_in, GATE_RANK),
            "w_out": _to_bf16(gla_w_out),
            "w_a2": jnp.pad(gla_w_a2, ((0, 0), (0, LANES - GATE_RANK), (0, 0))).astype(BF16),
            "b_a": gla_b_a[:, None, :], "hnorm_w": gla_hnorm_w[:, None, :],
        },
    }


def _prompt_trunk(x, w):
    batch, seq, _ = x.shape
    x = x.reshape(batch * seq, D_MODEL)
    new_c, new_n, new_m, new_conv, new_s = [], [], [], [], []
    for l in range(DEPTH):
        x1, qk, v, og, sm = _pre_call(x, w, l, cast_ahead=l + 1 < DEPTH)
        if l % 2 == 1:
            x, s_new = _scanpost_call(x1, qk, v, og, sm, w, l, batch, seq)
            new_s.append(s_new)
        else:
            x, conv_new, c_new, n_new, m_new = _scanpost_call(x1, qk, v, og, sm, w, l, batch, seq)
            new_c.append(c_new); new_conv.append(conv_new)
            new_n.append(n_new.reshape(batch, N_HEADS, DK)); new_m.append(m_new.reshape(batch, N_HEADS))
    return (x.reshape(batch, seq, D_MODEL), jnp.stack(new_c), jnp.stack(new_n), jnp.stack(new_m),
            jnp.stack(new_conv), jnp.stack(new_s))


def _sample_trunk(x, w, c_all, n_all, m_all, conv_all, s_all):
    batch, seq, _ = x.shape
    x = x.reshape(batch * seq, D_MODEL)
    new_n, new_m, new_conv = [], [], []
    c_stack = s_stack = None
    for l in range(DEPTH):
        j = l // 2
        x1, qk, v, og, sm = _pre_call(x, w, l)
        if l % 2 == 1:
            gated, s_stack = _gla_packed_call(qk, v, og, sm, w, j, batch, seq, s_all, s_stack)
        else:
            gated, conv_new, c_stack, n_new, m_new = _mlstm_packed_call(
                qk, v, og, sm, w, j, batch, seq, conv_all, c_all, n_all[j], m_all[j], c_stack)
            new_n.append(n_new); new_m.append(m_new); new_conv.append(conv_new)
        x = _post_call(x1, gated, w, l)
    return (x.reshape(batch, seq, D_MODEL), c_stack, jnp.stack(new_n), jnp.stack(new_m), jnp.stack(new_conv),
            s_stack)


def kernel(x_prompt, x_sample, state_mlstm_C, state_mlstm_n, state_mlstm_m, state_mlstm_conv, state_gla_S, norm_w, final_norm_w, ffn_w_gate, ffn_w_up, ffn_w_down, mlstm_w_in, mlstm_conv_w, mlstm_conv_b, mlstm_b_i, mlstm_b_f, mlstm_hnorm_w, mlstm_w_out, gla_w_in, gla_w_a2, gla_b_a, gla_hnorm_w, gla_w_out):
    w = _prepare_weights(norm_w, final_norm_w, ffn_w_gate, ffn_w_up, ffn_w_down,
                         mlstm_w_in, mlstm_conv_w, mlstm_conv_b, mlstm_b_i, mlstm_b_f, mlstm_hnorm_w, mlstm_w_out,
                         gla_w_in, gla_w_a2, gla_b_a, gla_hnorm_w, gla_w_out)
    y_p, c_p, n_p, m_p, conv_p, s_p = _prompt_trunk(x_prompt, w)
    y_s, c_s, n_s, m_s, conv_s, s_s = _sample_trunk(
        x_sample, w, state_mlstm_C, state_mlstm_n, state_mlstm_m, state_mlstm_conv, state_gla_S)
    return (y_p, y_s, c_p, n_p, m_p, conv_p, s_p, c_s, n_s, m_s, conv_s, s_s)
```

```python
import functools
import math

import jax
import jax.numpy as jnp
import numpy as np
from jax import lax
from jax.experimental import pallas as pl
from jax.experimental.pallas import tpu as pltpu

D_MODEL = 1024
DEPTH = 4
N_HEADS = 4
DK = 128
DV = 256
QK_W = N_HEADS * DK
V_W = N_HEADS * DV
CONV_W = 4
GATE_RANK = 16
GLA_NORMALIZER = 16.0
D_FF = 2816
EPS = 1e-6
PROJ_W = 2 * QK_W + 2 * V_W
LOG2E = math.log2(math.e)

LANES = 128
SUBLANES = 8
VMEM_LIMIT_BYTES = 58 * 1024 * 1024

TOKEN_TILE = 256
PRE_TILE = 512
FFN_COLS = 256
FUSED_TILE = 256
MLSTM_ROWS = 256
GLA_ROWS = 128
PACKED_ROWS = 128

BF16 = jnp.bfloat16
F32 = jnp.float32


def _resident(shape, index=None):
    index = tuple(index or ())
    block = (None,) * len(index) + tuple(shape)
    return pl.BlockSpec(block, lambda *_: index + (0,) * len(shape), pipeline_mode=pl.Buffered(1))


def _rms(x, w):
    return x * lax.rsqrt(jnp.mean(x * x, axis=-1, keepdims=True) + EPS) * w


def _silu(x):
    return x * jax.nn.sigmoid(x)


def _log_sigmoid(x):
    return jnp.minimum(x, 0.0) - jnp.log1p(jnp.exp(-jnp.abs(x)))


def _dot(a, b):
    return jnp.dot(a.astype(BF16), b.astype(BF16), preferred_element_type=F32)


def _dot_nt(a, b):
    return lax.dot_general(a.astype(BF16), b.astype(BF16), (((1,), (1,)), ((), ())), preferred_element_type=F32)


def _dot_tn(a, b):
    return lax.dot_general(a.astype(BF16), b.astype(BF16), (((0,), (0,)), ((), ())), preferred_element_type=F32)


def _seg_cumsum(x, seg):
    n = x.shape[0]
    row = lax.broadcasted_iota(jnp.int32, (n, n), 0)
    col = lax.broadcasted_iota(jnp.int32, (n, n), 1)
    shift = seg.bit_length() - 1
    tri = jnp.where((col <= row) & ((row >> shift) == (col >> shift)), 1.0, 0.0).astype(BF16)
    hi = x.astype(BF16)
    r1 = x - hi.astype(F32)
    mid = r1.astype(BF16)
    lo = (r1 - mid.astype(F32)).astype(BF16)
    return jnp.dot(jnp.concatenate([tri, tri, tri], axis=1), jnp.concatenate([hi, mid, lo], axis=0),
                   preferred_element_type=F32)


def _group_row(x, group, idx):
    n, c = x.shape
    if n == group:
        return x[idx:idx + 1, :]
    return jnp.concatenate(
        [jnp.broadcast_to(x[g * group + idx:g * group + idx + 1, :], (group, c)) for g in range(n // group)], axis=0)


def _group_max(x, group):
    n, c = x.shape
    if n == group:
        return jnp.max(x, axis=0, keepdims=True)
    return jnp.concatenate(
        [jnp.broadcast_to(jnp.max(x[g * group:(g + 1) * group, :], axis=0, keepdims=True), (group, c))
         for g in range(n // group)], axis=0)


def _col_to_row(col):
    n = col.shape[0]
    eye = lax.broadcasted_iota(jnp.int32, (n, n), 0) == lax.broadcasted_iota(jnp.int32, (n, n), 1)
    return jnp.sum(jnp.where(eye, col, 0.0), axis=0, keepdims=True)


def _row_to_col(row):
    n = row.shape[1]
    eye = lax.broadcasted_iota(jnp.int32, (n, n), 0) == lax.broadcasted_iota(jnp.int32, (n, n), 1)
    return jnp.sum(jnp.where(eye, row, 0.0), axis=1, keepdims=True)


def _keep(valid, new, old_ref_value):
    return new if valid is None else jnp.where(valid, new, old_ref_value)


def _cast_kernel(w_ref, o_ref):
    o_ref[...] = w_ref[...].astype(o_ref.dtype)


def _to_bf16(w, first=0, count=None):
    n, r, c = w.shape
    count = n - first if count is None else count
    return pl.pallas_call(
        _cast_kernel, grid=(count,),
        in_specs=[pl.BlockSpec((1, r, c), lambda i: (first + i, 0, 0))],
        out_specs=pl.BlockSpec((1, r, c), lambda i: (i, 0, 0)),
        out_shape=jax.ShapeDtypeStruct((count, r, c), BF16),
        compiler_params=pltpu.CompilerParams(dimension_semantics=("parallel",), vmem_limit_bytes=VMEM_LIMIT_BYTES),
        name="cast_bf16",
    )(w)


def _ffn(x, nw, wg_ref, wu_ref, wd_ref, h_ref):
    xn = _rms(x, nw).astype(BF16)
    for c in range(0, D_FF, FFN_COLS):
        g = jnp.dot(xn, wg_ref[:, c:c + FFN_COLS], preferred_element_type=F32)
        u = jnp.dot(xn, wu_ref[:, c:c + FFN_COLS], preferred_element_type=F32)
        h_ref[:, c:c + FFN_COLS] = (_silu(g) * u).astype(BF16)
    return x + 0.5 * jnp.dot(h_ref[...], wd_ref[...], preferred_element_type=F32)


def _pre_kernel(*refs, is_gla, cast_ahead):
    n_in = 9 if is_gla else 7
    if cast_ahead:
        for src, dst in zip(refs[n_in:n_in + 3], refs[n_in + 8:n_in + 11]):
            dst[...] = src[...].astype(dst.dtype)
        refs = refs[:n_in] + refs[n_in + 3:n_in + 8] + refs[-1:]
    if is_gla:
        (x_ref, nw_ref, wg_ref, wu_ref, wd_ref, win_ref, wsm_ref, wa2_ref, ba_ref,
         x1_ref, qk_ref, v_ref, og_ref, sm_ref, h_ref) = refs
    else:
        (x_ref, nw_ref, wg_ref, wu_ref, wd_ref, win_ref, wsm_ref,
         x1_ref, qk_ref, v_ref, og_ref, sm_ref, h_ref) = refs
    x1 = _ffn(x_ref[...], nw_ref[0:1, :], wg_ref, wu_ref, wd_ref, h_ref)
    x1_ref[...] = x1
    xn = _rms(x1, nw_ref[1:2, :]).astype(BF16)
    small = jnp.dot(xn, wsm_ref[...], preferred_element_type=F32)
    if is_gla:
        a = jnp.dot(small.astype(BF16), wa2_ref[...], preferred_element_type=F32) + ba_ref[...]
        sm_ref[...] = _log_sigmoid(a) * (1.0 / GLA_NORMALIZER)
    else:
        sm_ref[...] = small
    for i, p_ref in enumerate((qk_ref, v_ref, og_ref)):
        p_ref[...] = jnp.dot(xn, win_ref[:, i * D_MODEL:(i + 1) * D_MODEL],
                             preferred_element_type=F32).astype(p_ref.dtype)


def _pre_call(x, w, l, cast_ahead=False):
    m = x.shape[0]
    is_gla = l % 2 == 1
    j = l // 2
    steps = m // PRE_TILE
    tile = lambda width: pl.BlockSpec((PRE_TILE, width), lambda i: (i, 0))
    sm_w = QK_W if is_gla else LANES
    mix = w["gla"] if is_gla else w["mlstm"]
    in_specs = [tile(D_MODEL), _resident((3, D_MODEL), (l,)), _resident((D_MODEL, D_FF), (0,)),
                _resident((D_MODEL, D_FF), (0,)), _resident((D_FF, D_MODEL), (0,)),
                _resident((D_MODEL, PROJ_W), (j,)), _resident((D_MODEL, LANES), (j,))]
    args = [x, w["norm"], w["gate"][l], w["up"][l], w["down"][l], mix["w_in"], mix["w_small"]]
    if is_gla:
        in_specs += [_resident((LANES, QK_W), (j,)), _resident((1, QK_W), (j,))]
        args += [mix["w_a2"], mix["b_a"]]
    out_specs = [tile(D_MODEL), tile(D_MODEL), tile(D_MODEL), tile(D_MODEL), tile(sm_w)]
    out_shape = [jax.ShapeDtypeStruct((m, D_MODEL), F32), jax.ShapeDtypeStruct((m, D_MODEL), F32),
                 jax.ShapeDtypeStruct((m, D_MODEL), BF16), jax.ShapeDtypeStruct((m, D_MODEL), F32),
                 jax.ShapeDtypeStruct((m, sm_w), F32)]
    if cast_ahead:
        for src in w["ffn_f32"]:
            rows, cols = src.shape[0] // DEPTH, src.shape[1]
            slab = rows // steps
            in_specs.append(pl.BlockSpec((slab, cols), lambda i, first=(l + 1) * steps: (first + i, 0)))
            args.append(src)
            out_specs.append(pl.BlockSpec((slab, cols), lambda i: (i, 0)))
            out_shape.append(jax.ShapeDtypeStruct((rows, cols), BF16))
    outs = pl.pallas_call(
        functools.partial(_pre_kernel, is_gla=is_gla, cast_ahead=cast_ahead),
        grid=(steps,),
        in_specs=in_specs,
        out_specs=out_specs,
        out_shape=out_shape,
        scratch_shapes=[pltpu.VMEM((PRE_TILE, D_FF), BF16)],
        compiler_params=pltpu.CompilerParams(dimension_semantics=("parallel",), vmem_limit_bytes=VMEM_LIMIT_BYTES),
        name="pre_gla" if is_gla else "pre_mlstm",
    )(*args)
    if cast_ahead:
        for name, cast in zip(("gate", "up", "down"), outs[5:]):
            w[name][l + 1] = cast.reshape((2, -1, cast.shape[1]))
    return outs[:5]


def _post(x1, gated, post_refs, h_ref, final):
    wout_ref, nw_ref, wg_ref, wu_ref, wd_ref = post_refs[:5]
    x2 = x1 + jnp.dot(gated, wout_ref[...], preferred_element_type=F32)
    x3 = _ffn(x2, nw_ref[2:3, :], wg_ref, wu_ref, wd_ref, h_ref)
    return _rms(x3, post_refs[5][...]) if final else x3


def _post_specs(w, l):
    is_gla = l % 2 == 1
    j = l // 2
    mix = w["gla"] if is_gla else w["mlstm"]
    specs = [_resident((V_W, D_MODEL), (j,)), _resident((3, D_MODEL), (l,)), _resident((D_MODEL, D_FF), (1,)),
             _resident((D_MODEL, D_FF), (1,)), _resident((D_FF, D_MODEL), (1,))]
    args = [mix["w_out"], w["norm"], w["gate"][l], w["up"][l], w["down"][l]]
    if l == DEPTH - 1:
        specs.append(_resident((1, D_MODEL)))
        args.append(w["final"])
    return specs, args


def _post_kernel(*refs, final):
    x_ref, gated_ref = refs[:2]
    o_ref, h_ref = refs[-2:]
    o_ref[...] = _post(x_ref[...], gated_ref[...], refs[2:-2], h_ref, final)


def _post_call(x1, gated, w, l):
    m = x1.shape[0]
    final = l == DEPTH - 1
    tile = pl.BlockSpec((TOKEN_TILE, D_MODEL), lambda i: (i, 0))
    specs, args = _post_specs(w, l)
    return pl.pallas_call(
        functools.partial(_post_kernel, final=final),
        grid=(m // TOKEN_TILE,),
        in_specs=[tile, tile] + specs,
        out_specs=tile,
        out_shape=jax.ShapeDtypeStruct((m, D_MODEL), F32),
        scratch_shapes=[pltpu.VMEM((TOKEN_TILE, D_FF), BF16)],
        compiler_params=pltpu.CompilerParams(dimension_semantics=("parallel",), vmem_limit_bytes=VMEM_LIMIT_BYTES),
        name="post_final" if final else "post",
    )(x1, gated, *args)


def _mlstm_init(first, state_out, ext_ref):
    conv_out, c_ref, n_ref, m_ref = state_out

    @pl.when(first)
    def _():
        ext_ref[0, 0:SUBLANES, :] = jnp.zeros((SUBLANES, 2 * QK_W), F32)
        conv_out[...] = jnp.zeros(conv_out.shape, F32)
        c_ref[...] = jnp.zeros(c_ref.shape, F32)
        n_ref[...] = jnp.zeros(n_ref.shape, F32)
        m_ref[...] = jnp.zeros(m_ref.shape, F32)


def _mlstm_prologue(tile_refs, w_refs, state_in, state_out, ext_ref, *, rows, seg, chunk=None, valid=None):
    qk_ref, gt_ref = tile_refs[0], tile_refs[3]
    cw_ref, cb_ref, gb_ref = w_refs[:3]
    conv_out = state_out[0]
    carry = state_in is None
    nseg = rows // seg
    hist = SUBLANES

    pieces = []
    for i in range(nseg):
        if not carry:
            ext_ref[i, hist - (CONV_W - 1):hist, :] = state_in[0][i]
        ext_ref[i, hist:hist + seg, :] = qk_ref[i * seg:(i + 1) * seg, :]
        acc = cb_ref[...] + cw_ref[CONV_W - 1:CONV_W, :] * ext_ref[i, hist:hist + seg, :]
        for j in range(1, CONV_W):
            acc = acc + cw_ref[CONV_W - 1 - j:CONV_W - j, :] * ext_ref[i, hist - j:hist - j + seg, :]
        conv_out[i] = _keep(valid, ext_ref[i, hist + seg - (CONV_W - 1):hist + seg, :], conv_out[i])
        pieces.append(acc)
    if carry:
        ext_ref[0, 0:hist, :] = ext_ref[0, seg:seg + hist, :]
    qk = _silu(pieces[0] if nseg == 1 else jnp.concatenate(pieces, axis=0))

    gt = gt_ref[...] + gb_ref[...]
    k = qk[:, QK_W:] * (DK ** -0.5)
    lf = _log_sigmoid(gt)
    if chunk is None or chunk == seg:
        bcum = _seg_cumsum(lf, seg)
    else:
        bcum = jnp.concatenate([_seg_cumsum(lf[r0:r0 + chunk, :], chunk) for r0 in range(0, rows, chunk)], axis=0)
    return qk[:, :QK_W], k, qk[:, :QK_W].astype(BF16), k.astype(BF16), gt, bcum


def _mlstm_head(h, ctx, tile_refs, w_refs, state_in, state_out, out_ref, *, rows, seg, valid=None):
    q_all, k_all, qb_all, kb_all, gt, bcum = ctx
    v_ref, o_ref = tile_refs[1], tile_refs[2]
    hw_ref = w_refs[3]
    _, c_ref, n_ref, m_ref = state_out
    carry = state_in is None
    nseg = rows // seg
    row = lax.broadcasted_iota(jnp.int32, (rows, rows), 0)
    col = lax.broadcasted_iota(jnp.int32, (rows, rows), 1)
    shift = seg.bit_length() - 1
    causal = (col <= row) & ((row >> shift) == (col >> shift))

    q = q_all[:, h * DK:(h + 1) * DK]
    k = k_all[:, h * DK:(h + 1) * DK]
    qb = qb_all[:, h * DK:(h + 1) * DK]
    v = v_ref[:, h * DV:(h + 1) * DV]
    ic = gt[:, h:h + 1]
    b = bcum[:, N_HEADS + h:N_HEADS + h + 1]
    if carry:
        m_prev = m_ref[0, :, h:h + 1]
        n_prev = n_ref[0, :, h * DK:(h + 1) * DK]
    else:
        m_prev = state_in[3][:, h:h + 1]
        n_prev = state_in[2][:, h * DK:(h + 1) * DK]
    c_prev = [c_ref[0, h] if carry else state_in[1][i, h] for i in range(nseg)]

    qk_t = _dot_nt(qb, kb_all[:, h * DK:(h + 1) * DK])
    inter = [_dot(qb[i * seg:(i + 1) * seg, :], c_prev[i]) for i in range(nseg)]
    inter = inter[0] if nseg == 1 else jnp.concatenate(inter, axis=0)
    a_row = _col_to_row(ic - b)
    dm = jnp.where(causal, b + a_row, -jnp.inf)
    mt = jnp.maximum(b + m_prev, jnp.max(dm, axis=1, keepdims=True))
    w_inter = jnp.exp(b + m_prev - mt)
    p = jnp.exp(dm - mt)
    b_last = _group_row(b, seg, seg - 1)
    g = b_last - b + ic
    m_new = jnp.maximum(b_last + m_prev, _group_max(g, seg))
    sc_prev = jnp.exp(b_last + m_prev - m_new)
    kw = k * jnp.exp(g - m_new)
    kw_b = kw.astype(BF16)
    yield

    s = qk_t * p
    s_b = s.astype(BF16)
    yield

    intra = _dot(s_b, v)
    kv = [_dot_tn(kw_b[i * seg:(i + 1) * seg, :], v[i * seg:(i + 1) * seg, :]) for i in range(nseg)]
    yield

    num = intra + w_inter * inter
    den = jnp.sum(s, axis=1, keepdims=True) + w_inter * jnp.sum(q * n_prev, axis=1, keepdims=True)
    hh = num / jnp.maximum(jnp.abs(den), jnp.exp(-mt))
    for i in range(nseg):
        lo, hi = i * seg, (i + 1) * seg
        sc_i = sc_prev[lo:lo + 1, :]
        c_ref[i, h] = _keep(valid, sc_i * c_prev[i] + kv[i], c_prev[i])
        n_old = n_prev[lo:lo + 1, :]
        n_ref[i, :, h * DK:(h + 1) * DK] = _keep(
            valid, sc_i * n_old + jnp.sum(kw[lo:hi, :], axis=0, keepdims=True), n_old)
        m_ref[i, :, h:h + 1] = _keep(valid, m_new[lo:lo + 1, :], m_prev[lo:lo + 1, :])
    hn = _rms(hh, hw_ref[:, h * DV:(h + 1) * DV])
    out_ref[:, h * DV:(h + 1) * DV] = (hn * jax.nn.sigmoid(o_ref[:, h * DV:(h + 1) * DV])).astype(out_ref.dtype)
    yield


def _mlstm_weight_specs(w, j):
    mw = w["mlstm"]
    specs = [_resident((CONV_W, 2 * QK_W), (j,)), _resident((1, 2 * QK_W), (j,)), _resident((1, LANES), (j,)),
             _resident((1, V_W), (j,))]
    return specs, [mw["conv_w"], mw["conv_b"], mw["gate_b"], mw["hnorm_w"]]


def _mlstm_packed_kernel(*refs, seg, aliased):
    outs = list(refs[13:] if aliased else refs[12:])
    if not aliased:
        stack_ref = outs[2]
        stack_ref[1:] = jnp.zeros((stack_ref.shape[0] - 1,) + stack_ref.shape[1:], F32)
        outs[2] = stack_ref.at[0]
    ctx = _mlstm_prologue(refs[0:4], refs[4:8], refs[8:12], outs[1:5], outs[5], rows=PACKED_ROWS, seg=seg)
    for h in range(N_HEADS):
        for _ in _mlstm_head(h, ctx, refs[0:4], refs[4:8], refs[8:12], outs[1:5], outs[0], rows=PACKED_ROWS, seg=seg):
            pass


def _mlstm_packed_call(qk, v, o, gates, w, j, batch, seq, conv_all, c_all, n0, m0, c_stack):
    rows, seg = PACKED_ROWS, seq
    nseg = rows // seg
    tile = lambda width: pl.BlockSpec((rows, width), lambda b: (b, 0))
    per_seq = lambda shape: pl.BlockSpec((nseg,) + shape, lambda b: (b,) + (0,) * len(shape))
    layer_seq = lambda shape: pl.BlockSpec((None, nseg) + shape, lambda b: (j, b) + (0,) * len(shape))
    w_specs, w_args = _mlstm_weight_specs(w, j)
    in_specs = ([tile(2 * QK_W), tile(V_W), tile(V_W), tile(LANES)] + w_specs
                + [layer_seq((CONV_W - 1, 2 * QK_W)), layer_seq((N_HEADS, DK, DV)), tile(QK_W), tile(N_HEADS)])
    args = [qk, v, o, gates] + w_args + [conv_all, c_all, jnp.repeat(n0.reshape(batch, QK_W), seq, axis=0),
                                         jnp.repeat(m0, seq, axis=0)]
    aliases = {}
    if c_stack is None:
        c_spec = pl.BlockSpec((c_all.shape[0], nseg, N_HEADS, DK, DV), lambda b: (0, b, 0, 0, 0))
    else:
        c_spec = layer_seq((N_HEADS, DK, DV))
        in_specs.append(pl.BlockSpec(memory_space=pl.ANY))
        args.append(c_stack)
        aliases = {len(args) - 1: 2}
    out, conv_new, c_stack, n_new, m_new = pl.pallas_call(
        functools.partial(_mlstm_packed_kernel, seg=seg, aliased=bool(aliases)),
        grid=(batch * seq // rows,),
        in_specs=in_specs,
        out_specs=[tile(V_W), per_seq((CONV_W - 1, 2 * QK_W)), c_spec, per_seq((1, QK_W)), per_seq((1, N_HEADS))],
        out_shape=[jax.ShapeDtypeStruct((batch * seq, V_W), BF16),
                   jax.ShapeDtypeStruct((batch, CONV_W - 1, 2 * QK_W), F32),
                   jax.ShapeDtypeStruct(c_all.shape, F32),
                   jax.ShapeDtypeStruct((batch, 1, QK_W), F32),
                   jax.ShapeDtypeStruct((batch, 1, N_HEADS), F32)],
        scratch_shapes=[pltpu.VMEM((nseg, seg + SUBLANES, 2 * QK_W), F32)],
        input_output_aliases=aliases,
        compiler_params=pltpu.CompilerParams(dimension_semantics=("parallel",), vmem_limit_bytes=VMEM_LIMIT_BYTES),
        name="mlstm_packed",
    )(*args)
    return out, conv_new, c_stack, n_new.reshape(batch, N_HEADS, DK), m_new.reshape(batch, N_HEADS)


def _gla_init(first, s_ref):
    @pl.when(first)
    def _():
        s_ref[...] = jnp.zeros(s_ref.shape, F32)


def _gla_halves(seg):
    halves, half = [], seg // 2
    while half >= SUBLANES:
        halves.append(half)
        half //= 2
    return halves


def _gla_codes(rows, seg):
    sub = min(SUBLANES, seg)
    r = np.arange(rows)[:, None]
    c = np.arange(rows)[None, :]
    code = np.full((rows, rows), -1, np.int32)
    halves = _gla_halves(seg)
    for level, half in enumerate(halves):
        grp = 2 * half
        code[(r // grp == c // grp) & (r % grp >= half) & (c % grp < half)] = level
    inside = (r // sub == c // sub) & (c <= r)
    return np.where(inside, len(halves) + (c % sub), code).astype(np.int32)


def _gla_prologue(la_ref, seg):
    return _seg_cumsum(la_ref[...], seg) * LOG2E


def _gla_head(h, b2, tile_refs, hw_ref, code_ref, s_in, s_ref, out_ref, *, rows, seg, valid=None):
    qk_ref, v_ref, g_ref = tile_refs[:3]
    carry = s_in is None
    nseg = rows // seg
    sub = min(SUBLANES, seg)
    halves = _gla_halves(seg)
    code = code_ref[...]
    rowk = lax.broadcasted_iota(jnp.int32, (rows, DK), 0)

    q = qk_ref[:, h * DK:(h + 1) * DK] * (DK ** -0.5)
    k = qk_ref[:, QK_W + h * DK:QK_W + (h + 1) * DK]
    v = v_ref[:, h * DV:(h + 1) * DV]
    s_prev = [s_ref[0, h] if carry else s_in[i, h] for i in range(nseg)]

    qd = (q * jnp.exp2(b2)).astype(BF16)
    s_prev_b = [s.astype(BF16) for s in s_prev]
    xs = []
    for half in halves:
        second = (rowk & (2 * half - 1)) >= half
        ref = _group_row(b2, 2 * half, half - 1)
        xs.append((jnp.where(second, q, k) * jnp.exp2(-jnp.abs(b2 - ref))).astype(BF16))
    yield

    if nseg > 1:
        inter = jnp.concatenate([_dot(qd[i * seg:(i + 1) * seg, :], s_prev_b[i]) for i in range(nseg)], axis=0)
    cross = [_dot_nt(x, x) for x in xs]
    att = jnp.zeros((rows, rows), F32)
    for s in range(sub):
        dec = jnp.exp2(b2 - _group_row(b2, sub, s))
        wgt = jnp.sum(q * _group_row(k, sub, s) * dec, axis=1, keepdims=True)
        att = jnp.where(code == len(halves) + s, wgt, att)
    b_last = _group_row(b2, seg, seg - 1)
    k_dec = (k * jnp.exp2(b_last - b2)).astype(BF16)
    yield

    for level, prod in enumerate(cross):
        att = jnp.where(code == level, prod, att)
    att = att.astype(BF16)
    yield

    if nseg > 1:
        o = inter + _dot(att, v)
    else:
        o = jnp.dot(jnp.concatenate([qd, att], axis=1), jnp.concatenate([s_prev_b[0], v], axis=0),
                    preferred_element_type=F32)
    kv = [_dot_tn(k_dec[i * seg:(i + 1) * seg, :], v[i * seg:(i + 1) * seg, :]) for i in range(nseg)]
    yield

    for i in range(nseg):
        decay = _row_to_col(jnp.exp2(b_last[i * seg:i * seg + 1, :]))
        s_ref[i, h] = _keep(valid, decay * s_prev[i] + kv[i], s_prev[i])
    on = _rms(o, hw_ref[:, h * DV:(h + 1) * DV])
    out_ref[:, h * DV:(h + 1) * DV] = (on * _silu(g_ref[:, h * DV:(h + 1) * DV])).astype(out_ref.dtype)
    yield


def _gla_packed_kernel(qk_ref, v_ref, g_ref, la_ref, hw_ref, code_ref, s_in, *rest, seg, aliased):
    out_ref, s_ref = rest[-2:]
    if not aliased:
        s_ref[1:] = jnp.zeros((s_ref.shape[0] - 1,) + s_ref.shape[1:], F32)
        s_ref = s_ref.at[0]
    b2_all = _gla_prologue(la_ref, seg)
    for h in range(N_HEADS):
        for _ in _gla_head(h, b2_all[:, h * DK:(h + 1) * DK], (qk_ref, v_ref, g_ref), hw_ref, code_ref, s_in, s_ref,
                           out_ref, rows=PACKED_ROWS, seg=seg):
            pass


def _gla_packed_call(qk, v, g, la, w, j, batch, seq, s_all, s_stack):
    rows, seg = PACKED_ROWS, seq
    nseg = rows // seg
    tile = lambda width: pl.BlockSpec((rows, width), lambda b: (b, 0))
    s_spec = pl.BlockSpec((None, nseg, N_HEADS, DK, DV), lambda b: (j, b, 0, 0, 0))
    in_specs = [tile(2 * QK_W), tile(V_W), tile(V_W), tile(QK_W), _resident((1, V_W), (j,)),
                _resident((rows, rows)), s_spec]
    args = [qk, v, g, la, w["gla"]["hnorm_w"], _gla_codes(rows, seg), s_all]
    aliases = {}
    if s_stack is None:
        out_spec = pl.BlockSpec((s_all.shape[0], nseg, N_HEADS, DK, DV), lambda b: (0, b, 0, 0, 0))
    else:
        out_spec = s_spec
        in_specs.append(pl.BlockSpec(memory_space=pl.ANY))
        args.append(s_stack)
        aliases = {len(args) - 1: 1}
    return pl.pallas_call(
        functools.partial(_gla_packed_kernel, seg=seg, aliased=bool(aliases)),
        grid=(batch * seq // rows,),
        in_specs=in_specs,
        out_specs=[tile(V_W), out_spec],
        out_shape=[jax.ShapeDtypeStruct((batch * seq, V_W), BF16), jax.ShapeDtypeStruct(s_all.shape, F32)],
        input_output_aliases=aliases,
        compiler_params=pltpu.CompilerParams(dimension_semantics=("parallel",), vmem_limit_bytes=VMEM_LIMIT_BYTES),
        name="gla_packed",
    )(*args)


def _scanpost_kernel(*refs, is_gla, rows, tiles_per_seq, ntiles, final):
    n_scan_w = 2 if is_gla else 4
    n_post_w = 6 if final else 5
    n_state = 1 if is_gla else 4
    tile_refs = refs[0:4]
    scan_w = refs[4:4 + n_scan_w]
    x_ref = refs[4 + n_scan_w]
    wout_ref, nw_ref, wg_ref, wu_ref, wd_ref = refs[5 + n_scan_w:10 + n_scan_w]
    o_ref = refs[5 + n_scan_w + n_post_w]
    state_out = refs[6 + n_scan_w + n_post_w:6 + n_scan_w + n_post_w + n_state]
    scratch = refs[6 + n_scan_w + n_post_w + n_state:]
    gated_ref, h_ref = scratch[0], scratch[1]

    step = pl.program_id(0)
    valid = step < ntiles
    first = jnp.logical_and(valid, lax.rem(step, tiles_per_seq) == 0)

    @pl.when(step == 0)
    def _():
        gated_ref[...] = jnp.zeros(gated_ref.shape, gated_ref.dtype)

    if is_gla:
        _gla_init(first, state_out[0])
    else:
        _mlstm_init(first, state_out, scratch[2])

    def scan_parts():
        if is_gla:
            b2_all = {r0: _gla_prologue(tile_refs[3].at[r0:r0 + GLA_ROWS, :], GLA_ROWS)
                      for r0 in range(0, rows, GLA_ROWS)}
            yield
            for h in range(N_HEADS):
                for r0 in range(0, rows, GLA_ROWS):
                    part_refs = tuple(r.at[r0:r0 + GLA_ROWS, :] for r in tile_refs[:3])
                    yield from _gla_head(h, b2_all[r0][:, h * DK:(h + 1) * DK], part_refs, scan_w[0],
                                         scan_w[1], None, state_out[0], gated_ref.at[r0:r0 + GLA_ROWS, :],
                                         rows=GLA_ROWS, seg=GLA_ROWS, valid=valid)
        else:
            ctx = _mlstm_prologue(tile_refs, scan_w, None, state_out, scratch[2], rows=rows, seg=rows,
                                  chunk=MLSTM_ROWS, valid=valid)
            yield
            for r0 in range(0, rows, MLSTM_ROWS):
                part_ctx = tuple(a[r0:r0 + MLSTM_ROWS, :] for a in ctx)
                part_refs = tuple(r.at[r0:r0 + MLSTM_ROWS, :] for r in tile_refs)
                for h in range(N_HEADS):
                    yield from _mlstm_head(h, part_ctx, part_refs, scan_w, None, state_out,
                                           gated_ref.at[r0:r0 + MLSTM_ROWS, :], rows=MLSTM_ROWS, seg=MLSTM_ROWS,
                                           valid=valid)

    n_scan = 1 + N_HEADS * (5 * (rows // GLA_ROWS) if is_gla else 4 * (rows // MLSTM_ROWS))
    post = {}

    def out_proj():
        post["x2"] = x_ref[...] + jnp.dot(gated_ref[...], wout_ref[...], preferred_element_type=F32)
        post["xn"] = _rms(post["x2"], nw_ref[2:3, :]).astype(BF16)

    def ffn_cols(c):
        g = jnp.dot(post["xn"], wg_ref[:, c:c + FFN_COLS], preferred_element_type=F32)
        u = jnp.dot(post["xn"], wu_ref[:, c:c + FFN_COLS], preferred_element_type=F32)
        h_ref[:, c:c + FFN_COLS] = (_silu(g) * u).astype(BF16)

    def down_proj(lo, hi):
        part = jnp.dot(h_ref[:, lo:hi], wd_ref[lo:hi, :], preferred_element_type=F32)
        post["y"] = part if "y" not in post else post["y"] + part

    post_items = [out_proj]
    n_cols = D_FF // FFN_COLS
    for p in range(N_HEADS):
        lo, hi = FFN_COLS * (p * n_cols // N_HEADS), FFN_COLS * ((p + 1) * n_cols // N_HEADS)
        post_items += [functools.partial(ffn_cols, c) for c in range(lo, hi, FFN_COLS)]
        post_items.append(functools.partial(down_proj, lo, hi))

    lead = 2 if is_gla else 4
    post_items[0]()
    done = 1
    for i, _ in enumerate(scan_parts()):
        upto = lead + i * (len(post_items) - lead) // (n_scan - 1)
        for item in post_items[done:upto]:
            item()
        done = upto
    assert done == len(post_items)

    x3 = post["x2"] + 0.5 * post["y"]
    o_ref[...] = _rms(x3, refs[10 + n_scan_w][...]) if final else x3


def _scanpost_call(x1, qk, v, og, sm, w, l, batch, seq):
    is_gla = l % 2 == 1
    j = l // 2
    final = l == DEPTH - 1
    rows = FUSED_TILE
    tiles_per_seq = seq // rows
    ntiles = batch * tiles_per_seq
    cur = lambda width: pl.BlockSpec((rows, width), lambda s: (jnp.minimum(s, ntiles - 1), 0))
    prev = pl.BlockSpec((rows, D_MODEL), lambda s: (jnp.maximum(s - 1, 0), 0))
    per_seq = lambda shape: pl.BlockSpec(
        (1,) + shape, lambda s: (jnp.minimum(s, ntiles - 1) // tiles_per_seq,) + (0,) * len(shape))
    if is_gla:
        scan_specs = [_resident((1, V_W), (j,)), _resident((GLA_ROWS, GLA_ROWS))]
        scan_args = [w["gla"]["hnorm_w"], _gla_codes(GLA_ROWS, GLA_ROWS)]
        state_specs = [per_seq((N_HEADS, DK, DV))]
        state_shapes = [jax.ShapeDtypeStruct((batch, N_HEADS, DK, DV), F32)]
        scratch = []
        sm_w = QK_W
    else:
        scan_specs, scan_args = _mlstm_weight_specs(w, j)
        state_specs = [per_seq((CONV_W - 1, 2 * QK_W)), per_seq((N_HEADS, DK, DV)), per_seq((1, QK_W)),
                       per_seq((1, N_HEADS))]
        state_shapes = [jax.ShapeDtypeStruct((batch, CONV_W - 1, 2 * QK_W), F32),
                        jax.ShapeDtypeStruct((batch, N_HEADS, DK, DV), F32),
                        jax.ShapeDtypeStruct((batch, 1, QK_W), F32),
                        jax.ShapeDtypeStruct((batch, 1, N_HEADS), F32)]
        scratch = [pltpu.VMEM((1, rows + SUBLANES, 2 * QK_W), F32)]
        sm_w = LANES
    post_specs, post_args = _post_specs(w, l)
    return pl.pallas_call(
        functools.partial(_scanpost_kernel, is_gla=is_gla, rows=rows, tiles_per_seq=tiles_per_seq, ntiles=ntiles,
                          final=final),
        grid=(ntiles + 1,),
        in_specs=[cur(2 * QK_W), cur(V_W), cur(V_W), cur(sm_w)] + scan_specs + [prev] + post_specs,
        out_specs=[prev] + state_specs,
        out_shape=[jax.ShapeDtypeStruct((batch * seq, D_MODEL), F32)] + state_shapes,
        scratch_shapes=[pltpu.VMEM((rows, V_W), BF16), pltpu.VMEM((rows, D_FF), BF16)] + scratch,
        compiler_params=pltpu.CompilerParams(dimension_semantics=("arbitrary",), vmem_limit_bytes=VMEM_LIMIT_BYTES),
        name=("gla" if is_gla else "mlstm") + "_scanpost",
    )(qk, v, og, sm, *scan_args, x1, *post_args)


def _prepare_weights(norm_w, final_norm_w, ffn_w_gate, ffn_w_up, ffn_w_down,
                     mlstm_w_in, mlstm_conv_w, mlstm_conv_b, mlstm_b_i, mlstm_b_f, mlstm_hnorm_w, mlstm_w_out,
                     gla_w_in, gla_w_a2, gla_b_a, gla_hnorm_w, gla_w_out):
    def small(w_in, n):
        return jnp.pad(w_in[:, :, PROJ_W:], ((0, 0), (0, 0), (0, LANES - n))).astype(BF16)

    gate_b = jnp.pad(jnp.concatenate([mlstm_b_i, mlstm_b_f], axis=1), ((0, 0), (0, LANES - 2 * N_HEADS)))
    return {
        "norm": norm_w, "final": final_norm_w[None, :],
        "gate": [_to_bf16(ffn_w_gate.reshape(2 * DEPTH, D_MODEL, D_FF), 0, 2)] + [None] * (DEPTH - 1),
        "up": [_to_bf16(ffn_w_up.reshape(2 * DEPTH, D_MODEL, D_FF), 0, 2)] + [None] * (DEPTH - 1),
        "down": [_to_bf16(ffn_w_down.reshape(2 * DEPTH, D_FF, D_MODEL), 0, 2)] + [None] * (DEPTH - 1),
        "ffn_f32": [ffn_w_gate.reshape(-1, D_FF), ffn_w_up.reshape(-1, D_FF), ffn_w_down.reshape(-1, D_MODEL)],
        "mlstm": {
            "w_in": mlstm_w_in.astype(BF16)[:, :, :PROJ_W], "w_small": small(mlstm_w_in, 2 * N_HEADS),
            "w_out": _to_bf16(mlstm_w_out), "conv_w": mlstm_conv_w, "conv_b": mlstm_conv_b[:, None, :],
            "gate_b": gate_b[:, None, :], "hnorm_w": mlstm_hnorm_w[:, None, :],
        },
        "gla": {
            "w_in": gla_w_in.astype(BF16)[:, :, :PROJ_W], "w_small": small(gla_w_in, GATE_RANK),
            "w_out": _to_bf16(gla_w_out),
            "w_a2": jnp.pad(gla_w_a2, ((0, 0), (0, LANES - GATE_RANK), (0, 0))).astype(BF16),
            "b_a": gla_b_a[:, None, :], "hnorm_w": gla_hnorm_w[:, None, :],
        },
    }


def _prompt_trunk(x, w):
    batch, seq, _ = x.shape
    x = x.reshape(batch * seq, D_MODEL)
    new_c, new_n, new_m, new_conv, new_s = [], [], [], [], []
    for l in range(DEPTH):
        x1, qk, v, og, sm = _pre_call(x, w, l, cast_ahead=l + 1 < DEPTH)
        if l % 2 == 1:
            x, s_new = _scanpost_call(x1, qk, v, og, sm, w, l, batch, seq)
            new_s.append(s_new)
        else:
            x, conv_new, c_new, n_new, m_new = _scanpost_call(x1, qk, v, og, sm, w, l, batch, seq)
            new_c.append(c_new); new_conv.append(conv_new)
            new_n.append(n_new.reshape(batch, N_HEADS, DK)); new_m.append(m_new.reshape(batch, N_HEADS))
    return (x.reshape(batch, seq, D_MODEL), jnp.stack(new_c), jnp.stack(new_n), jnp.stack(new_m),
            jnp.stack(new_conv), jnp.stack(new_s))


def _sample_trunk(x, w, c_all, n_all, m_all, conv_all, s_all):
    batch, seq, _ = x.shape
    x = x.reshape(batch * seq, D_MODEL)
    new_n, new_m, new_conv = [], [], []
    c_stack = s_stack = None
    for l in range(DEPTH):
        j = l // 2
        x1, qk, v, og, sm = _pre_call(x, w, l)
        if l % 2 == 1:
            gated, s_stack = _gla_packed_call(qk, v, og, sm, w, j, batch, seq, s_all, s_stack)
        else:
            gated, conv_new, c_stack, n_new, m_new = _mlstm_packed_call(
                qk, v, og, sm, w, j, batch, seq, conv_all, c_all, n_all[j], m_all[j], c_stack)
            new_n.append(n_new); new_m.append(m_new); new_conv.append(conv_new)
        x = _post_call(x1, gated, w, l)
    return (x.reshape(batch, seq, D_MODEL), c_stack, jnp.stack(new_n), jnp.stack(new_m), jnp.stack(new_conv),
            s_stack)


def kernel(x_prompt, x_sample, state_mlstm_C, state_mlstm_n, state_mlstm_m, state_mlstm_conv, state_gla_S, norm_w, final_norm_w, ffn_w_gate, ffn_w_up, ffn_w_down, mlstm_w_in, mlstm_conv_w, mlstm_conv_b, mlstm_b_i, mlstm_b_f, mlstm_hnorm_w, mlstm_w_out, gla_w_in, gla_w_a2, gla_b_a, gla_hnorm_w, gla_w_out):
    w = _prepare_weights(norm_w, final_norm_w, ffn_w_gate, ffn_w_up, ffn_w_down,
                         mlstm_w_in, mlstm_conv_w, mlstm_conv_b, mlstm_b_i, mlstm_b_f, mlstm_hnorm_w, mlstm_w_out,
                         gla_w_in, gla_w_a2, gla_b_a, gla_hnorm_w, gla_w_out)
    y_p, c_p, n_p, m_p, conv_p, s_p = _prompt_trunk(x_prompt, w)
    y_s, c_s, n_s, m_s, conv_s, s_s = _sample_trunk(
        x_sample, w, state_mlstm_C, state_mlstm_n, state_mlstm_m, state_mlstm_conv, state_gla_S)
    return (y_p, y_s, c_p, n_p, m_p, conv_p, s_p, c_s, n_s, m_s, conv_s, s_s)
```

```python
import functools
import math

import jax
import jax.numpy as jnp
import numpy as np
from jax import lax
from jax.experimental import pallas as pl
from jax.experimental.pallas import tpu as pltpu

D_MODEL = 1024
DEPTH = 4
N_HEADS = 4
DK = 128
DV = 256
QK_W = N_HEADS * DK
V_W = N_HEADS * DV
CONV_W = 4
GATE_RANK = 16
GLA_NORMALIZER = 16.0
D_FF = 2816
EPS = 1e-6
PROJ_W = 2 * QK_W + 2 * V_W
LOG2E = math.log2(math.e)

LANES = 128
SUBLANES = 8
VMEM_LIMIT_BYTES = 58 * 1024 * 1024

TOKEN_TILE = 256
PRE_TILE = 512
FFN_COLS = 256
FUSED_TILE = 256
MLSTM_ROWS = 256
GLA_ROWS = 128
PACKED_ROWS = 64

BF16 = jnp.bfloat16
F32 = jnp.float32


def _resident(shape, index=None):
    index = tuple(index or ())
    block = (None,) * len(index) + tuple(shape)
    return pl.BlockSpec(block, lambda *_: index + (0,) * len(shape), pipeline_mode=pl.Buffered(1))


def _rms(x, w):
    return x * lax.rsqrt(jnp.mean(x * x, axis=-1, keepdims=True) + EPS) * w


def _silu(x):
    return x * jax.nn.sigmoid(x)


def _log_sigmoid(x):
    return jnp.minimum(x, 0.0) - jnp.log1p(jnp.exp(-jnp.abs(x)))


def _dot(a, b):
    return jnp.dot(a.astype(BF16), b.astype(BF16), preferred_element_type=F32)


def _dot_nt(a, b):
    return lax.dot_general(a.astype(BF16), b.astype(BF16), (((1,), (1,)), ((), ())), preferred_element_type=F32)


def _dot_tn(a, b):
    return lax.dot_general(a.astype(BF16), b.astype(BF16), (((0,), (0,)), ((), ())), preferred_element_type=F32)


def _seg_cumsum(x, seg):
    n = x.shape[0]
    row = lax.broadcasted_iota(jnp.int32, (n, n), 0)
    col = lax.broadcasted_iota(jnp.int32, (n, n), 1)
    shift = seg.bit_length() - 1
    tri = jnp.where((col <= row) & ((row >> shift) == (col >> shift)), 1.0, 0.0).astype(BF16)
    hi = x.astype(BF16)
    r1 = x - hi.astype(F32)
    mid = r1.astype(BF16)
    lo = (r1 - mid.astype(F32)).astype(BF16)
    return jnp.dot(jnp.concatenate([tri, tri, tri], axis=1), jnp.concatenate([hi, mid, lo], axis=0),
                   preferred_element_type=F32)


def _group_row(x, group, idx):
    n, c = x.shape
    if n == group:
        return x[idx:idx + 1, :]
    return jnp.concatenate(
        [jnp.broadcast_to(x[g * group + idx:g * group + idx + 1, :], (group, c)) for g in range(n // group)], axis=0)


def _group_max(x, group):
    n, c = x.shape
    if n == group:
        return jnp.max(x, axis=0, keepdims=True)
    return jnp.concatenate(
        [jnp.broadcast_to(jnp.max(x[g * group:(g + 1) * group, :], axis=0, keepdims=True), (group, c))
         for g in range(n // group)], axis=0)


def _col_to_row(col):
    n = col.shape[0]
    eye = lax.broadcasted_iota(jnp.int32, (n, n), 0) == lax.broadcasted_iota(jnp.int32, (n, n), 1)
    return jnp.sum(jnp.where(eye, col, 0.0), axis=0, keepdims=True)


def _row_to_col(row):
    n = row.shape[1]
    eye = lax.broadcasted_iota(jnp.int32, (n, n), 0) == lax.broadcasted_iota(jnp.int32, (n, n), 1)
    return jnp.sum(jnp.where(eye, row, 0.0), axis=1, keepdims=True)


def _keep(valid, new, old_ref_value):
    return new if valid is None else jnp.where(valid, new, old_ref_value)


def _cast_kernel(w_ref, o_ref):
    o_ref[...] = w_ref[...].astype(o_ref.dtype)


def _to_bf16(w, first=0, count=None):
    n, r, c = w.shape
    count = n - first if count is None else count
    return pl.pallas_call(
        _cast_kernel, grid=(count,),
        in_specs=[pl.BlockSpec((1, r, c), lambda i: (first + i, 0, 0))],
        out_specs=pl.BlockSpec((1, r, c), lambda i: (i, 0, 0)),
        out_shape=jax.ShapeDtypeStruct((count, r, c), BF16),
        compiler_params=pltpu.CompilerParams(dimension_semantics=("parallel",), vmem_limit_bytes=VMEM_LIMIT_BYTES),
        name="cast_bf16",
    )(w)


def _ffn(x, nw, wg_ref, wu_ref, wd_ref, h_ref):
    xn = _rms(x, nw).astype(BF16)
    for c in range(0, D_FF, FFN_COLS):
        g = jnp.dot(xn, wg_ref[:, c:c + FFN_COLS], preferred_element_type=F32)
        u = jnp.dot(xn, wu_ref[:, c:c + FFN_COLS], preferred_element_type=F32)
        h_ref[:, c:c + FFN_COLS] = (_silu(g) * u).astype(BF16)
    return x + 0.5 * jnp.dot(h_ref[...], wd_ref[...], preferred_element_type=F32)


def _pre_kernel(*refs, is_gla, cast_ahead):
    n_in = 9 if is_gla else 7
    if cast_ahead:
        for src, dst in zip(refs[n_in:n_in + 3], refs[n_in + 8:n_in + 11]):
            dst[...] = src[...].astype(dst.dtype)
        refs = refs[:n_in] + refs[n_in + 3:n_in + 8] + refs[-1:]
    if is_gla:
        (x_ref, nw_ref, wg_ref, wu_ref, wd_ref, win_ref, wsm_ref, wa2_ref, ba_ref,
         x1_ref, qk_ref, v_ref, og_ref, sm_ref, h_ref) = refs
    else:
        (x_ref, nw_ref, wg_ref, wu_ref, wd_ref, win_ref, wsm_ref,
         x1_ref, qk_ref, v_ref, og_ref, sm_ref, h_ref) = refs
    x1 = _ffn(x_ref[...], nw_ref[0:1, :], wg_ref, wu_ref, wd_ref, h_ref)
    x1_ref[...] = x1
    xn = _rms(x1, nw_ref[1:2, :]).astype(BF16)
    small = jnp.dot(xn, wsm_ref[...], preferred_element_type=F32)
    if is_gla:
        a = jnp.dot(small.astype(BF16), wa2_ref[...], preferred_element_type=F32) + ba_ref[...]
        sm_ref[...] = _log_sigmoid(a) * (1.0 / GLA_NORMALIZER)
    else:
        sm_ref[...] = small
    for i, p_ref in enumerate((qk_ref, v_ref, og_ref)):
        p_ref[...] = jnp.dot(xn, win_ref[:, i * D_MODEL:(i + 1) * D_MODEL],
                             preferred_element_type=F32).astype(p_ref.dtype)


def _pre_call(x, w, l, cast_ahead=False):
    m = x.shape[0]
    is_gla = l % 2 == 1
    j = l // 2
    steps = m // PRE_TILE
    tile = lambda width: pl.BlockSpec((PRE_TILE, width), lambda i: (i, 0))
    sm_w = QK_W if is_gla else LANES
    mix = w["gla"] if is_gla else w["mlstm"]
    in_specs = [tile(D_MODEL), _resident((3, D_MODEL), (l,)), _resident((D_MODEL, D_FF), (0,)),
                _resident((D_MODEL, D_FF), (0,)), _resident((D_FF, D_MODEL), (0,)),
                _resident((D_MODEL, PROJ_W), (j,)), _resident((D_MODEL, LANES), (j,))]
    args = [x, w["norm"], w["gate"][l], w["up"][l], w["down"][l], mix["w_in"], mix["w_small"]]
    if is_gla:
        in_specs += [_resident((LANES, QK_W), (j,)), _resident((1, QK_W), (j,))]
        args += [mix["w_a2"], mix["b_a"]]
    out_specs = [tile(D_MODEL), tile(D_MODEL), tile(D_MODEL), tile(D_MODEL), tile(sm_w)]
    out_shape = [jax.ShapeDtypeStruct((m, D_MODEL), F32), jax.ShapeDtypeStruct((m, D_MODEL), F32),
                 jax.ShapeDtypeStruct((m, D_MODEL), BF16), jax.ShapeDtypeStruct((m, D_MODEL), F32),
                 jax.ShapeDtypeStruct((m, sm_w), F32)]
    if cast_ahead:
        for src in w["ffn_f32"]:
            rows, cols = src.shape[0] // DEPTH, src.shape[1]
            slab = rows // steps
            in_specs.append(pl.BlockSpec((slab, cols), lambda i, first=(l + 1) * steps: (first + i, 0)))
            args.append(src)
            out_specs.append(pl.BlockSpec((slab, cols), lambda i: (i, 0)))
            out_shape.append(jax.ShapeDtypeStruct((rows, cols), BF16))
    outs = pl.pallas_call(
        functools.partial(_pre_kernel, is_gla=is_gla, cast_ahead=cast_ahead),
        grid=(steps,),
        in_specs=in_specs,
        out_specs=out_specs,
        out_shape=out_shape,
        scratch_shapes=[pltpu.VMEM((PRE_TILE, D_FF), BF16)],
        compiler_params=pltpu.CompilerParams(dimension_semantics=("parallel",), vmem_limit_bytes=VMEM_LIMIT_BYTES),
        name="pre_gla" if is_gla else "pre_mlstm",
    )(*args)
    if cast_ahead:
        for name, cast in zip(("gate", "up", "down"), outs[5:]):
            w[name][l + 1] = cast.reshape((2, -1, cast.shape[1]))
    return outs[:5]


def _post(x1, gated, post_refs, h_ref, final):
    wout_ref, nw_ref, wg_ref, wu_ref, wd_ref = post_refs[:5]
    x2 = x1 + jnp.dot(gated, wout_ref[...], preferred_element_type=F32)
    x3 = _ffn(x2, nw_ref[2:3, :], wg_ref, wu_ref, wd_ref, h_ref)
    return _rms(x3, post_refs[5][...]) if final else x3


def _post_specs(w, l):
    is_gla = l % 2 == 1
    j = l // 2
    mix = w["gla"] if is_gla else w["mlstm"]
    specs = [_resident((V_W, D_MODEL), (j,)), _resident((3, D_MODEL), (l,)), _resident((D_MODEL, D_FF), (1,)),
             _resident((D_MODEL, D_FF), (1,)), _resident((D_FF, D_MODEL), (1,))]
    args = [mix["w_out"], w["norm"], w["gate"][l], w["up"][l], w["down"][l]]
    if l == DEPTH - 1:
        specs.append(_resident((1, D_MODEL)))
        args.append(w["final"])
    return specs, args


def _post_kernel(*refs, final):
    x_ref, gated_ref = refs[:2]
    o_ref, h_ref = refs[-2:]
    o_ref[...] = _post(x_ref[...], gated_ref[...], refs[2:-2], h_ref, final)


def _post_call(x1, gated, w, l):
    m = x1.shape[0]
    final = l == DEPTH - 1
    tile = pl.BlockSpec((TOKEN_TILE, D_MODEL), lambda i: (i, 0))
    specs, args = _post_specs(w, l)
    return pl.pallas_call(
        functools.partial(_post_kernel, final=final),
        grid=(m // TOKEN_TILE,),
        in_specs=[tile, tile] + specs,
        out_specs=tile,
        out_shape=jax.ShapeDtypeStruct((m, D_MODEL), F32),
        scratch_shapes=[pltpu.VMEM((TOKEN_TILE, D_FF), BF16)],
        compiler_params=pltpu.CompilerParams(dimension_semantics=("parallel",), vmem_limit_bytes=VMEM_LIMIT_BYTES),
        name="post_final" if final else "post",
    )(x1, gated, *args)


def _mlstm_init(first, state_out, ext_ref):
    conv_out, c_ref, n_ref, m_ref = state_out

    @pl.when(first)
    def _():
        ext_ref[0, 0:SUBLANES, :] = jnp.zeros((SUBLANES, 2 * QK_W), F32)
        conv_out[...] = jnp.zeros(conv_out.shape, F32)
        c_ref[...] = jnp.zeros(c_ref.shape, F32)
        n_ref[...] = jnp.zeros(n_ref.shape, F32)
        m_ref[...] = jnp.zeros(m_ref.shape, F32)


def _mlstm_prologue(tile_refs, w_refs, state_in, state_out, ext_ref, *, rows, seg, chunk=None, valid=None):
    qk_ref, gt_ref = tile_refs[0], tile_refs[3]
    cw_ref, cb_ref, gb_ref = w_refs[:3]
    conv_out = state_out[0]
    carry = state_in is None
    nseg = rows // seg
    hist = SUBLANES

    pieces = []
    for i in range(nseg):
        if not carry:
            ext_ref[i, hist - (CONV_W - 1):hist, :] = state_in[0][i]
        ext_ref[i, hist:hist + seg, :] = qk_ref[i * seg:(i + 1) * seg, :]
        acc = cb_ref[...] + cw_ref[CONV_W - 1:CONV_W, :] * ext_ref[i, hist:hist + seg, :]
        for j in range(1, CONV_W):
            acc = acc + cw_ref[CONV_W - 1 - j:CONV_W - j, :] * ext_ref[i, hist - j:hist - j + seg, :]
        conv_out[i] = _keep(valid, ext_ref[i, hist + seg - (CONV_W - 1):hist + seg, :], conv_out[i])
        pieces.append(acc)
    if carry:
        ext_ref[0, 0:hist, :] = ext_ref[0, seg:seg + hist, :]
    qk = _silu(pieces[0] if nseg == 1 else jnp.concatenate(pieces, axis=0))

    gt = gt_ref[...] + gb_ref[...]
    k = qk[:, QK_W:] * (DK ** -0.5)
    lf = _log_sigmoid(gt)
    if chunk is None or chunk == seg:
        bcum = _seg_cumsum(lf, seg)
    else:
        bcum = jnp.concatenate([_seg_cumsum(lf[r0:r0 + chunk, :], chunk) for r0 in range(0, rows, chunk)], axis=0)
    return qk[:, :QK_W], k, qk[:, :QK_W].astype(BF16), k.astype(BF16), gt, bcum


def _mlstm_head(h, ctx, tile_refs, w_refs, state_in, state_out, out_ref, *, rows, seg, valid=None):
    q_all, k_all, qb_all, kb_all, gt, bcum = ctx
    v_ref, o_ref = tile_refs[1], tile_refs[2]
    hw_ref = w_refs[3]
    _, c_ref, n_ref, m_ref = state_out
    carry = state_in is None
    nseg = rows // seg
    row = lax.broadcasted_iota(jnp.int32, (rows, rows), 0)
    col = lax.broadcasted_iota(jnp.int32, (rows, rows), 1)
    shift = seg.bit_length() - 1
    causal = (col <= row) & ((row >> shift) == (col >> shift))

    q = q_all[:, h * DK:(h + 1) * DK]
    k = k_all[:, h * DK:(h + 1) * DK]
    qb = qb_all[:, h * DK:(h + 1) * DK]
    v = v_ref[:, h * DV:(h + 1) * DV]
    ic = gt[:, h:h + 1]
    b = bcum[:, N_HEADS + h:N_HEADS + h + 1]
    if carry:
        m_prev = m_ref[0, :, h:h + 1]
        n_prev = n_ref[0, :, h * DK:(h + 1) * DK]
    else:
        m_prev = state_in[3][:, h:h + 1]
        n_prev = state_in[2][:, h * DK:(h + 1) * DK]
    c_prev = [c_ref[0, h] if carry else state_in[1][i, h] for i in range(nseg)]

    qk_t = _dot_nt(qb, kb_all[:, h * DK:(h + 1) * DK])
    inter = [_dot(qb[i * seg:(i + 1) * seg, :], c_prev[i]) for i in range(nseg)]
    inter = inter[0] if nseg == 1 else jnp.concatenate(inter, axis=0)
    a_row = _col_to_row(ic - b)
    dm = jnp.where(causal, b + a_row, -jnp.inf)
    mt = jnp.maximum(b + m_prev, jnp.max(dm, axis=1, keepdims=True))
    w_inter = jnp.exp(b + m_prev - mt)
    p = jnp.exp(dm - mt)
    b_last = _group_row(b, seg, seg - 1)
    g = b_last - b + ic
    m_new = jnp.maximum(b_last + m_prev, _group_max(g, seg))
    sc_prev = jnp.exp(b_last + m_prev - m_new)
    kw = k * jnp.exp(g - m_new)
    kw_b = kw.astype(BF16)
    yield

    s = qk_t * p
    s_b = s.astype(BF16)
    yield

    intra = _dot(s_b, v)
    kv = [_dot_tn(kw_b[i * seg:(i + 1) * seg, :], v[i * seg:(i + 1) * seg, :]) for i in range(nseg)]
    yield

    num = intra + w_inter * inter
    den = jnp.sum(s, axis=1, keepdims=True) + w_inter * jnp.sum(q * n_prev, axis=1, keepdims=True)
    hh = num / jnp.maximum(jnp.abs(den), jnp.exp(-mt))
    for i in range(nseg):
        lo, hi = i * seg, (i + 1) * seg
        sc_i = sc_prev[lo:lo + 1, :]
        c_ref[i, h] = _keep(valid, sc_i * c_prev[i] + kv[i], c_prev[i])
        n_old = n_prev[lo:lo + 1, :]
        n_ref[i, :, h * DK:(h + 1) * DK] = _keep(
            valid, sc_i * n_old + jnp.sum(kw[lo:hi, :], axis=0, keepdims=True), n_old)
        m_ref[i, :, h:h + 1] = _keep(valid, m_new[lo:lo + 1, :], m_prev[lo:lo + 1, :])
    hn = _rms(hh, hw_ref[:, h * DV:(h + 1) * DV])
    out_ref[:, h * DV:(h + 1) * DV] = (hn * jax.nn.sigmoid(o_ref[:, h * DV:(h + 1) * DV])).astype(out_ref.dtype)
    yield


def _mlstm_weight_specs(w, j):
    mw = w["mlstm"]
    specs = [_resident((CONV_W, 2 * QK_W), (j,)), _resident((1, 2 * QK_W), (j,)), _resident((1, LANES), (j,)),
             _resident((1, V_W), (j,))]
    return specs, [mw["conv_w"], mw["conv_b"], mw["gate_b"], mw["hnorm_w"]]


def _mlstm_packed_kernel(*refs, seg, aliased):
    outs = list(refs[13:] if aliased else refs[12:])
    if not aliased:
        stack_ref = outs[2]
        stack_ref[1:] = jnp.zeros((stack_ref.shape[0] - 1,) + stack_ref.shape[1:], F32)
        outs[2] = stack_ref.at[0]
    ctx = _mlstm_prologue(refs[0:4], refs[4:8], refs[8:12], outs[1:5], outs[5], rows=PACKED_ROWS, seg=seg)
    for h in range(N_HEADS):
        for _ in _mlstm_head(h, ctx, refs[0:4], refs[4:8], refs[8:12], outs[1:5], outs[0], rows=PACKED_ROWS, seg=seg):
            pass


def _mlstm_packed_call(qk, v, o, gates, w, j, batch, seq, conv_all, c_all, n0, m0, c_stack):
    rows, seg = PACKED_ROWS, seq
    nseg = rows // seg
    tile = lambda width: pl.BlockSpec((rows, width), lambda b: (b, 0))
    per_seq = lambda shape: pl.BlockSpec((nseg,) + shape, lambda b: (b,) + (0,) * len(shape))
    layer_seq = lambda shape: pl.BlockSpec((None, nseg) + shape, lambda b: (j, b) + (0,) * len(shape))
    w_specs, w_args = _mlstm_weight_specs(w, j)
    in_specs = ([tile(2 * QK_W), tile(V_W), tile(V_W), tile(LANES)] + w_specs
                + [layer_seq((CONV_W - 1, 2 * QK_W)), layer_seq((N_HEADS, DK, DV)), tile(QK_W), tile(N_HEADS)])
    args = [qk, v, o, gates] + w_args + [conv_all, c_all, jnp.repeat(n0.reshape(batch, QK_W), seq, axis=0),
                                         jnp.repeat(m0, seq, axis=0)]
    aliases = {}
    if c_stack is None:
        c_spec = pl.BlockSpec((c_all.shape[0], nseg, N_HEADS, DK, DV), lambda b: (0, b, 0, 0, 0))
    else:
        c_spec = layer_seq((N_HEADS, DK, DV))
        in_specs.append(pl.BlockSpec(memory_space=pl.ANY))
        args.append(c_stack)
        aliases = {len(args) - 1: 2}
    out, conv_new, c_stack, n_new, m_new = pl.pallas_call(
        functools.partial(_mlstm_packed_kernel, seg=seg, aliased=bool(aliases)),
        grid=(batch * seq // rows,),
        in_specs=in_specs,
        out_specs=[tile(V_W), per_seq((CONV_W - 1, 2 * QK_W)), c_spec, per_seq((1, QK_W)), per_seq((1, N_HEADS))],
        out_shape=[jax.ShapeDtypeStruct((batch * seq, V_W), BF16),
                   jax.ShapeDtypeStruct((batch, CONV_W - 1, 2 * QK_W), F32),
                   jax.ShapeDtypeStruct(c_all.shape, F32),
                   jax.ShapeDtypeStruct((batch, 1, QK_W), F32),
                   jax.ShapeDtypeStruct((batch, 1, N_HEADS), F32)],
        scratch_shapes=[pltpu.VMEM((nseg, seg + SUBLANES, 2 * QK_W), F32)],
        input_output_aliases=aliases,
        compiler_params=pltpu.CompilerParams(dimension_semantics=("parallel",), vmem_limit_bytes=VMEM_LIMIT_BYTES),
        name="mlstm_packed",
    )(*args)
    return out, conv_new, c_stack, n_new.reshape(batch, N_HEADS, DK), m_new.reshape(batch, N_HEADS)


def _gla_init(first, s_ref):
    @pl.when(first)
    def _():
        s_ref[...] = jnp.zeros(s_ref.shape, F32)


def _gla_halves(seg):
    halves, half = [], seg // 2
    while half >= SUBLANES:
        halves.append(half)
        half //= 2
    return halves


def _gla_codes(rows, seg):
    sub = min(SUBLANES, seg)
    r = np.arange(rows)[:, None]
    c = np.arange(rows)[None, :]
    code = np.full((rows, rows), -1, np.int32)
    halves = _gla_halves(seg)
    for level, half in enumerate(halves):
        grp = 2 * half
        code[(r // grp == c // grp) & (r % grp >= half) & (c % grp < half)] = level
    inside = (r // sub == c // sub) & (c <= r)
    return np.where(inside, len(halves) + (c % sub), code).astype(np.int32)


def _gla_prologue(la_ref, seg):
    return _seg_cumsum(la_ref[...], seg) * LOG2E


def _gla_head(h, b2, tile_refs, hw_ref, code_ref, s_in, s_ref, out_ref, *, rows, seg, valid=None):
    qk_ref, v_ref, g_ref = tile_refs[:3]
    carry = s_in is None
    nseg = rows // seg
    sub = min(SUBLANES, seg)
    halves = _gla_halves(seg)
    code = code_ref[...]
    rowk = lax.broadcasted_iota(jnp.int32, (rows, DK), 0)

    q = qk_ref[:, h * DK:(h + 1) * DK] * (DK ** -0.5)
    k = qk_ref[:, QK_W + h * DK:QK_W + (h + 1) * DK]
    v = v_ref[:, h * DV:(h + 1) * DV]
    s_prev = [s_ref[0, h] if carry else s_in[i, h] for i in range(nseg)]

    qd = (q * jnp.exp2(b2)).astype(BF16)
    s_prev_b = [s.astype(BF16) for s in s_prev]
    xs = []
    for half in halves:
        second = (rowk & (2 * half - 1)) >= half
        ref = _group_row(b2, 2 * half, half - 1)
        xs.append((jnp.where(second, q, k) * jnp.exp2(-jnp.abs(b2 - ref))).astype(BF16))
    yield

    if nseg > 1:
        inter = jnp.concatenate([_dot(qd[i * seg:(i + 1) * seg, :], s_prev_b[i]) for i in range(nseg)], axis=0)
    cross = [_dot_nt(x, x) for x in xs]
    wgts = []
    for s in range(sub):
        dec = jnp.exp2(b2 - _group_row(b2, sub, s))
        wgts.append(jnp.sum(q * _group_row(k, sub, s) * dec, axis=1, keepdims=True))
    b_last = _group_row(b2, seg, seg - 1)
    k_dec = (k * jnp.exp2(b_last - b2)).astype(BF16)
    yield

    att = jnp.zeros((rows, rows), F32)
    for level, prod in enumerate(cross):
        att = jnp.where(code == level, prod, att)
    for s, wgt in enumerate(wgts):
        att = jnp.where(code == len(halves) + s, wgt, att)
    att = att.astype(BF16)
    yield

    if nseg > 1:
        o = inter + _dot(att, v)
    else:
        o = jnp.dot(jnp.concatenate([qd, att], axis=1), jnp.concatenate([s_prev_b[0], v], axis=0),
                    preferred_element_type=F32)
    kv = [_dot_tn(k_dec[i * seg:(i + 1) * seg, :], v[i * seg:(i + 1) * seg, :]) for i in range(nseg)]
    yield

    for i in range(nseg):
        decay = _row_to_col(jnp.exp2(b_last[i * seg:i * seg + 1, :]))
        s_ref[i, h] = _keep(valid, decay * s_prev[i] + kv[i], s_prev[i])
    on = _rms(o, hw_ref[:, h * DV:(h + 1) * DV])
    out_ref[:, h * DV:(h + 1) * DV] = (on * _silu(g_ref[:, h * DV:(h + 1) * DV])).astype(out_ref.dtype)
    yield


def _gla_packed_kernel(qk_ref, v_ref, g_ref, la_ref, hw_ref, code_ref, s_in, *rest, seg, aliased):
    out_ref, s_ref = rest[-2:]
    if not aliased:
        s_ref[1:] = jnp.zeros((s_ref.shape[0] - 1,) + s_ref.shape[1:], F32)
        s_ref = s_ref.at[0]
    b2_all = _gla_prologue(la_ref, seg)
    for h in range(N_HEADS):
        for _ in _gla_head(h, b2_all[:, h * DK:(h + 1) * DK], (qk_ref, v_ref, g_ref), hw_ref, code_ref, s_in, s_ref,
                           out_ref, rows=PACKED_ROWS, seg=seg):
            pass


def _gla_packed_call(qk, v, g, la, w, j, batch, seq, s_all, s_stack):
    rows, seg = PACKED_ROWS, seq
    nseg = rows // seg
    tile = lambda width: pl.BlockSpec((rows, width), lambda b: (b, 0))
    s_spec = pl.BlockSpec((None, nseg, N_HEADS, DK, DV), lambda b: (j, b, 0, 0, 0))
    in_specs = [tile(2 * QK_W), tile(V_W), tile(V_W), tile(QK_W), _resident((1, V_W), (j,)),
                _resident((rows, rows)), s_spec]
    args = [qk, v, g, la, w["gla"]["hnorm_w"], _gla_codes(rows, seg), s_all]
    aliases = {}
    if s_stack is None:
        out_spec = pl.BlockSpec((s_all.shape[0], nseg, N_HEADS, DK, DV), lambda b: (0, b, 0, 0, 0))
    else:
        out_spec = s_spec
        in_specs.append(pl.BlockSpec(memory_space=pl.ANY))
        args.append(s_stack)
        aliases = {len(args) - 1: 1}
    return pl.pallas_call(
        functools.partial(_gla_packed_kernel, seg=seg, aliased=bool(aliases)),
        grid=(batch * seq // rows,),
        in_specs=in_specs,
        out_specs=[tile(V_W), out_spec],
        out_shape=[jax.ShapeDtypeStruct((batch * seq, V_W), BF16), jax.ShapeDtypeStruct(s_all.shape, F32)],
        input_output_aliases=aliases,
        compiler_params=pltpu.CompilerParams(dimension_semantics=("parallel",), vmem_limit_bytes=VMEM_LIMIT_BYTES),
        name="gla_packed",
    )(*args)


def _scanpost_kernel(*refs, is_gla, rows, tiles_per_seq, ntiles, final):
    n_scan_w = 2 if is_gla else 4
    n_post_w = 6 if final else 5
    n_state = 1 if is_gla else 4
    tile_refs = refs[0:4]
    scan_w = refs[4:4 + n_scan_w]
    x_ref = refs[4 + n_scan_w]
    wout_ref, nw_ref, wg_ref, wu_ref, wd_ref = refs[5 + n_scan_w:10 + n_scan_w]
    o_ref = refs[5 + n_scan_w + n_post_w]
    state_out = refs[6 + n_scan_w + n_post_w:6 + n_scan_w + n_post_w + n_state]
    scratch = refs[6 + n_scan_w + n_post_w + n_state:]
    gated_ref, h_ref = scratch[0], scratch[1]

    step = pl.program_id(0)
    valid = step < ntiles
    first = jnp.logical_and(valid, lax.rem(step, tiles_per_seq) == 0)

    @pl.when(step == 0)
    def _():
        gated_ref[...] = jnp.zeros(gated_ref.shape, gated_ref.dtype)

    if is_gla:
        _gla_init(first, state_out[0])
    else:
        _mlstm_init(first, state_out, scratch[2])

    def scan_parts():
        if is_gla:
            b2_all = {r0: _gla_prologue(tile_refs[3].at[r0:r0 + GLA_ROWS, :], GLA_ROWS)
                      for r0 in range(0, rows, GLA_ROWS)}
            yield
            for h in range(N_HEADS):
                for r0 in range(0, rows, GLA_ROWS):
                    part_refs = tuple(r.at[r0:r0 + GLA_ROWS, :] for r in tile_refs[:3])
                    yield from _gla_head(h, b2_all[r0][:, h * DK:(h + 1) * DK], part_refs, scan_w[0],
                                         scan_w[1], None, state_out[0], gated_ref.at[r0:r0 + GLA_ROWS, :],
                                         rows=GLA_ROWS, seg=GLA_ROWS, valid=valid)
        else:
            ctx = _mlstm_prologue(tile_refs, scan_w, None, state_out, scratch[2], rows=rows, seg=rows,
                                  chunk=MLSTM_ROWS, valid=valid)
            yield
            for r0 in range(0, rows, MLSTM_ROWS):
                part_ctx = tuple(a[r0:r0 + MLSTM_ROWS, :] for a in ctx)
                part_refs = tuple(r.at[r0:r0 + MLSTM_ROWS, :] for r in tile_refs)
                for h in range(N_HEADS):
                    yield from _mlstm_head(h, part_ctx, part_refs, scan_w, None, state_out,
                                           gated_ref.at[r0:r0 + MLSTM_ROWS, :], rows=MLSTM_ROWS, seg=MLSTM_ROWS,
                                           valid=valid)

    n_scan = 1 + N_HEADS * (5 * (rows // GLA_ROWS) if is_gla else 4 * (rows // MLSTM_ROWS))
    post = {}

    def out_proj():
        post["x2"] = x_ref[...] + jnp.dot(gated_ref[...], wout_ref[...], preferred_element_type=F32)
        post["xn"] = _rms(post["x2"], nw_ref[2:3, :]).astype(BF16)

    def ffn_cols(c):
        g = jnp.dot(post["xn"], wg_ref[:, c:c + FFN_COLS], preferred_element_type=F32)
        u = jnp.dot(post["xn"], wu_ref[:, c:c + FFN_COLS], preferred_element_type=F32)
        h_ref[:, c:c + FFN_COLS] = (_silu(g) * u).astype(BF16)

    def down_proj(lo, hi):
        part = jnp.dot(h_ref[:, lo:hi], wd_ref[lo:hi, :], preferred_element_type=F32)
        post["y"] = part if "y" not in post else post["y"] + part

    post_items = [out_proj]
    n_cols = D_FF // FFN_COLS
    for p in range(N_HEADS):
        lo, hi = FFN_COLS * (p * n_cols // N_HEADS), FFN_COLS * ((p + 1) * n_cols // N_HEADS)
        post_items += [functools.partial(ffn_cols, c) for c in range(lo, hi, FFN_COLS)]
        post_items.append(functools.partial(down_proj, lo, hi))

    lead = 2 if is_gla else 4
    post_items[0]()
    done = 1
    for i, _ in enumerate(scan_parts()):
        upto = lead + i * (len(post_items) - lead) // (n_scan - 1)
        for item in post_items[done:upto]:
            item()
        done = upto
    assert done == len(post_items)

    x3 = post["x2"] + 0.5 * post["y"]
    o_ref[...] = _rms(x3, refs[10 + n_scan_w][...]) if final else x3


def _scanpost_call(x1, qk, v, og, sm, w, l, batch, seq):
    is_gla = l % 2 == 1
    j = l // 2
    final = l == DEPTH - 1
    rows = FUSED_TILE
    tiles_per_seq = seq // rows
    ntiles = batch * tiles_per_seq
    cur = lambda width: pl.BlockSpec((rows, width), lambda s: (jnp.minimum(s, ntiles - 1), 0))
    prev = pl.BlockSpec((rows, D_MODEL), lambda s: (jnp.maximum(s - 1, 0), 0))
    per_seq = lambda shape: pl.BlockSpec(
        (1,) + shape, lambda s: (jnp.minimum(s, ntiles - 1) // tiles_per_seq,) + (0,) * len(shape))
    if is_gla:
        scan_specs = [_resident((1, V_W), (j,)), _resident((GLA_ROWS, GLA_ROWS))]
        scan_args = [w["gla"]["hnorm_w"], _gla_codes(GLA_ROWS, GLA_ROWS)]
        state_specs = [per_seq((N_HEADS, DK, DV))]
        state_shapes = [jax.ShapeDtypeStruct((batch, N_HEADS, DK, DV), F32)]
        scratch = []
        sm_w = QK_W
    else:
        scan_specs, scan_args = _mlstm_weight_specs(w, j)
        state_specs = [per_seq((CONV_W - 1, 2 * QK_W)), per_seq((N_HEADS, DK, DV)), per_seq((1, QK_W)),
                       per_seq((1, N_HEADS))]
        state_shapes = [jax.ShapeDtypeStruct((batch, CONV_W - 1, 2 * QK_W), F32),
                        jax.ShapeDtypeStruct((batch, N_HEADS, DK, DV), F32),
                        jax.ShapeDtypeStruct((batch, 1, QK_W), F32),
                        jax.ShapeDtypeStruct((batch, 1, N_HEADS), F32)]
        scratch = [pltpu.VMEM((1, rows + SUBLANES, 2 * QK_W), F32)]
        sm_w = LANES
    post_specs, post_args = _post_specs(w, l)
    return pl.pallas_call(
        functools.partial(_scanpost_kernel, is_gla=is_gla, rows=rows, tiles_per_seq=tiles_per_seq, ntiles=ntiles,
                          final=final),
        grid=(ntiles + 1,),
        in_specs=[cur(2 * QK_W), cur(V_W), cur(V_W), cur(sm_w)] + scan_specs + [prev] + post_specs,
        out_specs=[prev] + state_specs,
        out_shape=[jax.ShapeDtypeStruct((batch * seq, D_MODEL), F32)] + state_shapes,
        scratch_shapes=[pltpu.VMEM((rows, V_W), BF16), pltpu.VMEM((rows, D_FF), BF16)] + scratch,
        compiler_params=pltpu.CompilerParams(dimension_semantics=("arbitrary",), vmem_limit_bytes=VMEM_LIMIT_BYTES),
        name=("gla" if is_gla else "mlstm") + "_scanpost",
    )(qk, v, og, sm, *scan_args, x1, *post_args)


def _prepare_weights(norm_w, final_norm_w, ffn_w_gate, ffn_w_up, ffn_w_down,
                     mlstm_w_in, mlstm_conv_w, mlstm_conv_b, mlstm_b_i, mlstm_b_f, mlstm_hnorm_w, mlstm_w_out,
                     gla_w_in, gla_w_a2, gla_b_a, gla_hnorm_w, gla_w_out):
    def small(w_in, n):
        return jnp.pad(w_in[:, :, PROJ_W:], ((0, 0), (0, 0), (0, LANES - n))).astype(BF16)

    gate_b = jnp.pad(jnp.concatenate([mlstm_b_i, mlstm_b_f], axis=1), ((0, 0), (0, LANES - 2 * N_HEADS)))
    return {
        "norm": norm_w, "final": final_norm_w[None, :],
        "gate": [_to_bf16(ffn_w_gate.reshape(2 * DEPTH, D_MODEL, D_FF), 0, 2)] + [None] * (DEPTH - 1),
        "up": [_to_bf16(ffn_w_up.reshape(2 * DEPTH, D_MODEL, D_FF), 0, 2)] + [None] * (DEPTH - 1),
        "down": [_to_bf16(ffn_w_down.reshape(2 * DEPTH, D_FF, D_MODEL), 0, 2)] + [None] * (DEPTH - 1),
        "ffn_f32": [ffn_w_gate.reshape(-1, D_FF), ffn_w_up.reshape(-1, D_FF), ffn_w_down.reshape(-1, D_MODEL)],
        "mlstm": {
            "w_in": mlstm_w_in.astype(BF16)[:, :, :PROJ_W], "w_small": small(mlstm_w_in, 2 * N_HEADS),
            "w_out": _to_bf16(mlstm_w_out), "conv_w": mlstm_conv_w, "conv_b": mlstm_conv_b[:, None, :],
            "gate_b": gate_b[:, None, :], "hnorm_w": mlstm_hnorm_w[:, None, :],
        },
        "gla": {
            "w_in": gla_w_in.astype(BF16)[:, :, :PROJ_W], "w_small": small(gla_w_in, GATE_RANK),
            "w_out": _to_bf16(gla_w_out),
            "w_a2": jnp.pad(gla_w_a2, ((0, 0), (0, LANES - GATE_RANK), (0, 0))).astype(BF16),
            "b_a": gla_b_a[:, None, :], "hnorm_w": gla_hnorm_w[:, None, :],
        },
    }


def _prompt_trunk(x, w):
    batch, seq, _ = x.shape
    x = x.reshape(batch * seq, D_MODEL)
    new_c, new_n, new_m, new_conv, new_s = [], [], [], [], []
    for l in range(DEPTH):
        x1, qk, v, og, sm = _pre_call(x, w, l, cast_ahead=l + 1 < DEPTH)
        if l % 2 == 1:
            x, s_new = _scanpost_call(x1, qk, v, og, sm, w, l, batch, seq)
            new_s.append(s_new)
        else:
            x, conv_new, c_new, n_new, m_new = _scanpost_call(x1, qk, v, og, sm, w, l, batch, seq)
            new_c.append(c_new); new_conv.append(conv_new)
            new_n.append(n_new.reshape(batch, N_HEADS, DK)); new_m.append(m_new.reshape(batch, N_HEADS))
    return (x.reshape(batch, seq, D_MODEL), jnp.stack(new_c), jnp.stack(new_n), jnp.stack(new_m),
            jnp.stack(new_conv), jnp.stack(new_s))


def _sample_trunk(x, w, c_all, n_all, m_all, conv_all, s_all):
    batch, seq, _ = x.shape
    x = x.reshape(batch * seq, D_MODEL)
    new_n, new_m, new_conv = [], [], []
    c_stack = s_stack = None
    for l in range(DEPTH):
        j = l // 2
        x1, qk, v, og, sm = _pre_call(x, w, l)
        if l % 2 == 1:
            gated, s_stack = _gla_packed_call(qk, v, og, sm, w, j, batch, seq, s_all, s_stack)
        else:
            gated, conv_new, c_stack, n_new, m_new = _mlstm_packed_call(
                qk, v, og, sm, w, j, batch, seq, conv_all, c_all, n_all[j], m_all[j], c_stack)
            new_n.append(n_new); new_m.append(m_new); new_conv.append(conv_new)
        x = _post_call(x1, gated, w, l)
    return (x.reshape(batch, seq, D_MODEL), c_stack, jnp.stack(new_n), jnp.stack(new_m), jnp.stack(new_conv),
            s_stack)


def kernel(x_prompt, x_sample, state_mlstm_C, state_mlstm_n, state_mlstm_m, state_mlstm_conv, state_gla_S, norm_w, final_norm_w, ffn_w_gate, ffn_w_up, ffn_w_down, mlstm_w_in, mlstm_conv_w, mlstm_conv_b, mlstm_b_i, mlstm_b_f, mlstm_hnorm_w, mlstm_w_out, gla_w_in, gla_w_a2, gla_b_a, gla_hnorm_w, gla_w_out):
    w = _prepare_weights(norm_w, final_norm_w, ffn_w_gate, ffn_w_up, ffn_w_down,
                         mlstm_w_in, mlstm_conv_w, mlstm_conv_b, mlstm_b_i, mlstm_b_f, mlstm_hnorm_w, mlstm_w_out,
                         gla_w_in, gla_w_a2, gla_b_a, gla_hnorm_w, gla_w_out)
    y_p, c_p, n_p, m_p, conv_p, s_p = _prompt_trunk(x_prompt, w)
    y_s, c_s, n_s, m_s, conv_s, s_s = _sample_trunk(
        x_sample, w, state_mlstm_C, state_mlstm_n, state_mlstm_m, state_mlstm_conv, state_gla_S)
    return (y_p, y_s, c_p, n_p, m_p, conv_p, s_p, c_s, n_s, m_s, conv_s, s_s)
```
